```python
import math
import jax, jax.numpy as jnp
from jax import lax
import numpy as np

D_MODEL = 1024
BATCH = 8
SEQ = 4096
DEPTH = 2

N_MIXERS = 2
N_A = (DEPTH + 1) // 2
N_B = DEPTH // 2
HG_HEAD_DIM = 128
HG_HEADS = D_MODEL // HG_HEAD_DIM
HG_DIM = HG_HEADS * HG_HEAD_DIM
HG_CHUNK = 64
D_INNER = 2 * D_MODEL
SSM_HEAD_DIM = 64
SSM_HEADS = D_INNER // SSM_HEAD_DIM
SSM_GROUPS = 8
SSM_HPG = SSM_HEADS // SSM_GROUPS
SSM_STATE = 128
SSM_CONV = 5
SSD_CHUNK = 128
GN = SSM_GROUPS * SSM_STATE
CONV_DIM = D_INNER + 2 * GN
B_PROJ = 2 * D_INNER + 2 * GN + 2 * SSM_HEADS
D_FF = ((8 * D_MODEL // 3 + 255) // 256) * 256
FFN_CONV = 3
EPS = 1e-6

kernel_name = "hgrn2_mamba2_convglu_bidir_hybrid"


def rmsnorm(x, w):
    xf = x.astype(jnp.float32)
    y = xf * lax.rsqrt(jnp.mean(xf * xf, axis=-1, keepdims=True) + EPS)
    return (y * w.astype(jnp.float32)).astype(x.dtype)


def group_rmsnorm(x, w, groups):
    shp = x.shape
    xf = x.astype(jnp.float32).reshape(*shp[:-1], groups, shp[-1] // groups)
    y = xf * lax.rsqrt(jnp.mean(xf * xf, axis=-1, keepdims=True) + EPS)
    y = y.reshape(shp) * w.astype(jnp.float32)
    return y.astype(x.dtype)


def dwconv_centred(x, w, b):
    ch = x.shape[-1]
    y = lax.conv_general_dilated(x, w[:, None, :].astype(x.dtype), window_strides=(1,), padding='SAME',
                                 dimension_numbers=('NWC', 'WIO', 'NWC'), feature_group_count=ch)
    return y + b.astype(x.dtype)


def flip(t):
    return jnp.flip(t, axis=1)


def gla_chunked(q, k, v, logf):
    bsz, seq, h, dk = q.shape
    dv = v.shape[-1]
    nc = seq // HG_CHUNK

    def chunks(t):
        return t.reshape(bsz, nc, HG_CHUNK, h, t.shape[-1]).transpose(1, 0, 3, 2, 4)

    lower = jnp.tril(jnp.ones((HG_CHUNK, HG_CHUNK), dtype=bool))

    def step(state, inp):
        qc, kc, vc, gc = inp
        b = jnp.cumsum(gc, axis=2)
        o_inter = jnp.einsum('bhtk,bhkv->bhtv', qc * jnp.exp(b), state)
        rel = jnp.where(lower[:, :, None], b[:, :, :, None, :] - b[:, :, None, :, :], -jnp.inf)
        att = jnp.einsum('bhtsk,bhsk->bhts', qc[:, :, :, None, :] * jnp.exp(rel), kc)
        o = o_inter + jnp.einsum('bhts,bhsv->bhtv', att, vc)
        b_last = b[:, :, -1:, :]
        state = (jnp.exp(b_last)[:, :, 0, :, None] * state
                 + jnp.einsum('bhsk,bhsv->bhkv', kc * jnp.exp(b_last - b), vc))
        return state, o

    state0 = jnp.zeros((bsz, h, dk, dv), q.dtype)
    _, o = lax.scan(step, state0, (chunks(q), chunks(k), chunks(v), chunks(logf)))
    return o.transpose(1, 0, 3, 2, 4).reshape(bsz, seq, h, dv)


def hgrn2_mixer(u, w_in, lb, norm_w, w_out):
    bsz, seq, _ = u.shape
    q, f_fw, f_bw, iv, g = jnp.split(u @ w_in, 5, axis=-1)

    def heads(t):
        return t.reshape(bsz, seq, HG_HEADS, HG_HEAD_DIM)

    def gate(fr):
        f = lb + (1.0 - lb) * jax.nn.sigmoid(fr.astype(jnp.float32))
        return heads(jnp.log(f)).astype(u.dtype), heads(1.0 - f).astype(u.dtype)

    q = heads(jax.nn.silu(q))
    iv = heads(iv)
    logf_fw, k_fw = gate(f_fw)
    logf_bw, k_bw = gate(f_bw)
    o = gla_chunked(q, k_fw, iv, logf_fw) + flip(gla_chunked(flip(q), flip(k_bw), flip(iv), flip(logf_bw)))
    o = rmsnorm(o, norm_w) * jax.nn.silu(heads(g))
    return o.reshape(bsz, seq, HG_DIM) @ w_out


def ssd_chunked(x, dt, a, bm, cm):
    bsz, seq = x.shape[:2]
    nc = seq // SSD_CHUNK
    c = SSD_CHUNK
    xd = (x * dt[..., None]).reshape(bsz, nc, c, SSM_GROUPS, SSM_HPG, SSM_HEAD_DIM)
    la = (dt * a).reshape(bsz, nc, c, SSM_GROUPS, SSM_HPG).transpose(0, 3, 4, 1, 2)
    a_cum = jnp.cumsum(la, axis=-1)
    bc = bm.reshape(bsz, nc, c, SSM_GROUPS, SSM_STATE)
    cc = cm.reshape(bsz, nc, c, SSM_GROUPS, SSM_STATE)
    lower = jnp.tril(jnp.ones((c, c), dtype=bool))
    lmat = jnp.exp(jnp.where(lower, a_cum[..., :, None] - a_cum[..., None, :], -jnp.inf))
    cb = jnp.einsum('bclgn,bcsgn->bgcls', cc, bc)
    y_diag = jnp.einsum('bgjcls,bcsgjp->bclgjp', cb[:, :, None] * lmat, xd)
    decay_s = jnp.exp(a_cum[..., -1:] - a_cum).transpose(0, 3, 4, 1, 2)[..., None]
    states = jnp.einsum('bcsgn,bcsgjp->bcgjpn', bc, xd * decay_s)
    a_last = a_cum[..., -1]
    a_cs = jnp.cumsum(a_last, axis=-1)
    a_excl = a_cs - a_last
    before = jnp.tril(jnp.ones((nc, nc), dtype=bool), k=-1)
    w = jnp.exp(jnp.where(before, a_excl[..., :, None] - a_cs[..., None, :], -jnp.inf))
    h_in = jnp.einsum('bgjzc,bcgjpn->bzgjpn', w, states)
    y_off = (jnp.einsum('bzlgn,bzgjpn->bzlgjp', cc, h_in)
             * jnp.exp(a_cum).transpose(0, 3, 4, 1, 2)[..., None])
    return (y_diag + y_off).reshape(bsz, seq, SSM_GROUPS, SSM_HPG, SSM_HEAD_DIM)


def mamba2_mixer(u, w_in, conv_w, conv_b, dt_bias, a_log, d_skip, norm_w, w_out):
    bsz, seq, _ = u.shape
    z, xbc, dt_raw = jnp.split(u @ w_in, [D_INNER, D_INNER + CONV_DIM], axis=-1)
    xbc = jax.nn.silu(dwconv_centred(xbc, conv_w, conv_b))
    xs, bm, cm = jnp.split(xbc, [D_INNER, D_INNER + GN], axis=-1)
    xs = xs.reshape(bsz, seq, SSM_GROUPS, SSM_HPG, SSM_HEAD_DIM)
    bm = bm.reshape(bsz, seq, SSM_GROUPS, SSM_STATE)
    cm = cm.reshape(bsz, seq, SSM_GROUPS, SSM_STATE)
    dt = jax.nn.softplus(dt_raw.reshape(bsz, seq, 2, SSM_HEADS) + dt_bias)
    a = -jnp.exp(a_log).reshape(2, SSM_GROUPS, SSM_HPG)
    dt_fw = dt[:, :, 0].reshape(bsz, seq, SSM_GROUPS, SSM_HPG)
    dt_bw = dt[:, :, 1].reshape(bsz, seq, SSM_GROUPS, SSM_HPG)
    y = (ssd_chunked(xs, dt_fw, a[0], bm, cm)
         + flip(ssd_chunked(flip(xs), flip(dt_bw), a[1], flip(bm), flip(cm)))
         + xs * d_skip.reshape(SSM_GROUPS, SSM_HPG, 1))
    y = y.reshape(bsz, seq, D_INNER) * jax.nn.silu(z)
    y = group_rmsnorm(y, norm_w, SSM_GROUPS)
    return y @ w_out


def conv_glu(u, w_in, conv_w, conv_b, w_out):
    gate, val = jnp.split(u @ w_in, 2, axis=-1)
    return (jax.nn.silu(dwconv_centred(gate, conv_w, conv_b)) * val) @ w_out


def setup_inputs(seed: int = 0) -> dict:
    key = jax.random.key(seed)
    ks = jax.random.split(key, 20)

    def nrm(k, shape, scale):
        return scale * jax.random.normal(k, shape, jnp.float32)

    dt = jnp.exp(jax.random.uniform(ks[10], (N_B, 2, SSM_HEADS), jnp.float32)
                 * (math.log(0.1) - math.log(1e-3)) + math.log(1e-3))
    return {
        "x": nrm(ks[0], (BATCH, SEQ, D_MODEL), 1.0),
        "norm1_w": 1.0 + nrm(ks[1], (DEPTH, D_MODEL), 0.02),
        "norm2_w": 1.0 + nrm(ks[2], (DEPTH, D_MODEL), 0.02),
        "a_w_in": nrm(ks[3], (N_A, D_MODEL, 5 * HG_DIM), D_MODEL ** -0.5),
        "a_lb_logits": nrm(ks[4], (DEPTH + 1, HG_DIM), 0.1),
        "a_norm_w": 1.0 + nrm(ks[5], (N_A, HG_HEAD_DIM), 0.02),
        "a_w_out": nrm(ks[6], (N_A, HG_DIM, D_MODEL), HG_DIM ** -0.5),
        "b_w_in": nrm(ks[7], (N_B, D_MODEL, B_PROJ), D_MODEL ** -0.5),
        "b_conv_w": nrm(ks[8], (N_B, SSM_CONV, CONV_DIM), SSM_CONV ** -0.5),
        "b_conv_b": nrm(ks[9], (N_B, CONV_DIM), 0.02),
        "b_dt_bias": dt + jnp.log(-jnp.expm1(-dt)),
        "b_a_log": jnp.log(jax.random.uniform(ks[11], (N_B, 2, SSM_HEADS), jnp.float32, 1.0, 16.0)),
        "b_d_skip": 1.0 + nrm(ks[12], (N_B, SSM_HEADS), 0.02),
        "b_norm_w": 1.0 + nrm(ks[13], (N_B, D_INNER), 0.02),
        "b_w_out": nrm(ks[14], (N_B, D_INNER, D_MODEL), D_INNER ** -0.5),
        "ffn_w_in": nrm(ks[15], (DEPTH, D_MODEL, 2 * D_FF), D_MODEL ** -0.5),
        "ffn_conv_w": nrm(ks[16], (DEPTH, FFN_CONV, D_FF), FFN_CONV ** -0.5),
        "ffn_conv_b": nrm(ks[17], (DEPTH, D_FF), 0.02),
        "ffn_w_out": nrm(ks[18], (DEPTH, D_FF, D_MODEL), D_FF ** -0.5),
        "final_norm_w": 1.0 + nrm(ks[19], (D_MODEL,), 0.02),
    }


def reference(x, norm1_w, norm2_w, a_w_in, a_lb_logits, a_norm_w, a_w_out, b_w_in, b_conv_w, b_conv_b,
              b_dt_bias, b_a_log, b_d_skip, b_norm_w, b_w_out, ffn_w_in, ffn_conv_w, ffn_conv_b, ffn_w_out,
              final_norm_w):
    lower_bounds = jnp.cumsum(jax.nn.softmax(a_lb_logits.astype(jnp.float32), axis=0), axis=0)
    h = x
    for i in range(DEPTH):
        u = rmsnorm(h, norm1_w[i])
        j = i // N_MIXERS
        if i % N_MIXERS == 0:
            h = h + hgrn2_mixer(u, a_w_in[j], lower_bounds[i], a_norm_w[j], a_w_out[j])
        else:
            h = h + mamba2_mixer(u, b_w_in[j], b_conv_w[j], b_conv_b[j], b_dt_bias[j], b_a_log[j],
                                 b_d_skip[j], b_norm_w[j], b_w_out[j])
        h = h + conv_glu(rmsnorm(h, norm2_w[i]), ffn_w_in[i], ffn_conv_w[i], ffn_conv_b[i], ffn_w_out[i])
    return rmsnorm(h, final_norm_w)
```

```python
import functools

import jax
import jax.numpy as jnp
from jax import lax
from jax.experimental import pallas as pl
from jax.experimental.pallas import tpu as pltpu

F32 = jnp.float32
BF16 = jnp.bfloat16

EPS = 1e-6
HG_HEAD_DIM = 128
SSM_HEAD_DIM = 64
SSM_GROUPS = 8
SSM_CONV_HALO = 2
FFN_CONV_HALO = 1
SUBLANES = 8
LANES = 128
GLA_CHUNK = 64
SSD_CHUNK = 128
GLA_SAFE_LOG_DECAY = 80.0
VMEM_LIMIT_BYTES = 56 * 1024 * 1024


def _cparams(*sem):
    return pltpu.CompilerParams(dimension_semantics=sem, vmem_limit_bytes=VMEM_LIMIT_BYTES)


def _const_spec(shape):
    return pl.BlockSpec(shape, lambda *_: (0,) * len(shape), pipeline_mode=pl.Buffered(1))


def _rmsnorm(x, w):
    return x * lax.rsqrt(jnp.mean(x * x, axis=-1, keepdims=True) + EPS) * w


def _sigmoid(x):
    return 1.0 / (1.0 + jnp.exp(-x))


def _silu(x):
    return x * _sigmoid(x)


def _softplus(x):
    return jnp.maximum(x, 0.0) + jnp.log(1.0 + jnp.exp(-jnp.abs(x)))


def _dot(a, b):
    return jnp.dot(a, b, preferred_element_type=F32)


def _dot_nt(a, b):
    return lax.dot_general(a, b, (((1,), (1,)), ((), ())), preferred_element_type=F32)


def _dot_tn(a, b):
    return lax.dot_general(a, b, (((0,), (0,)), ((), ())), preferred_element_type=F32)


def _split3(x):
    hi = x.astype(BF16)
    r1 = x - hi.astype(F32)
    mid = r1.astype(BF16)
    lo = (r1 - mid.astype(F32)).astype(BF16)
    return hi, mid, lo


def _sel_rows(m01, x):
    hi, mid, lo = _split3(x)
    return _dot(m01, hi) + _dot(m01, mid) + _dot(m01, lo)


def _sel_cols(x, m01):
    hi, mid, lo = _split3(x)
    return _dot(hi, m01) + _dot(mid, m01) + _dot(lo, m01)


def _iota2(shape, dim):
    return lax.broadcasted_iota(jnp.int32, shape, dim)


def _hgrn_proj_kernel(h_ref, nw_ref, w_ref, lbl_ref, q_ref, lff_ref, lfb_ref, v_ref, g_ref,
                      *, layer, col_chunk):
    d = h_ref.shape[1]
    u = _rmsnorm(h_ref[...], nw_ref[...]).astype(BF16)
    lg = lbl_ref[...]
    n_rows = lg.shape[0]
    mx = lg[0:1]
    for r in range(1, n_rows):
        mx = jnp.maximum(mx, lg[r:r + 1])
    es = [jnp.exp(lg[r:r + 1] - mx) for r in range(n_rows)]
    tot = es[0]
    for r in range(1, n_rows):
        tot = tot + es[r]
    part = es[0]
    for r in range(1, layer + 1):
        part = part + es[r]
    lb = part / tot

    def logf(fr, cols):
        lbc = lb[:, cols]
        return jnp.log(lbc + (1.0 - lbc) * _sigmoid(fr))

    for sec in range(5):
        for c0 in range(0, d, col_chunk):
            cols = slice(c0, c0 + col_chunk)
            p = _dot(u, w_ref[:, sec * d + c0: sec * d + c0 + col_chunk])
            if sec == 0:
                q_ref[:, cols] = _silu(p).astype(BF16)
            elif sec == 1:
                lff_ref[:, cols] = logf(p, cols)
            elif sec == 2:
                lfb_ref[:, cols] = logf(p, cols)
            elif sec == 3:
                v_ref[:, cols] = p.astype(BF16)
            else:
                g_ref[:, cols] = _silu(p).astype(BF16)


def _hgrn_proj(h2, norm_w, w_in_bf, lb_logits, layer, tm):
    t, d = h2.shape
    tok = lambda dt: jax.ShapeDtypeStruct((t, d), dt)
    tile = pl.BlockSpec((tm, d), lambda i: (i, 0))
    return pl.pallas_call(
        functools.partial(_hgrn_proj_kernel, layer=layer, col_chunk=min(512, d)),
        grid=(t // tm,),
        in_specs=[tile, _const_spec((1, d)), _const_spec(w_in_bf.shape), _const_spec(lb_logits.shape)],
        out_specs=[tile] * 5,
        out_shape=[tok(BF16), tok(F32), tok(F32), tok(BF16), tok(BF16)],
        compiler_params=_cparams("parallel"),
        name="hgrn_proj",
    )(h2, norm_w.reshape(1, d), w_in_bf, lb_logits)


def _gla_step(q_ref, lf_ref, v_ref, o_ref, st_ref, *, reverse, safe, b_all):
    c = q_ref.shape[1]
    n_heads = st_ref.shape[0]
    hd = st_ref.shape[2]
    row = _iota2((c, c), 0)
    col = _iota2((c, c), 1)
    last = 0 if reverse else c - 1
    for h in range(n_heads):
        hs = slice(h * hd, (h + 1) * hd)
        b = b_all[:, hs]
        bl = b[last:last + 1, :]
        qf = q_ref[0, :, hs].astype(F32)
        kf = 1.0 - jnp.exp(lf_ref[0, :, hs])
        vb = v_ref[0, :, hs]
        q_in = qf * jnp.exp(b)
        k_out = (kf * jnp.exp(bl - b)).astype(BF16)
        st = st_ref[h]
        o = _dot_nt(q_in.astype(BF16), st.astype(BF16))
        if safe:
            q_hat = (q_in * jnp.exp(-bl)).astype(BF16)
            keep = (row <= col) if reverse else (row >= col)
            att = jnp.where(keep, _dot_nt(q_hat, k_out), 0.0)
        else:
            att = jnp.where(row == col, _dot_nt(qf.astype(BF16), kf.astype(BF16)), 0.0)
            lvl = 0
            while (1 << lvl) < c:
                bs = 1 << lvl
                edge = (row >> (lvl + 1)) * (2 * bs) + (bs if reverse else bs - 1)
                sel = (col == edge).astype(BF16)
                g = _sel_rows(sel, b)
                q_l = (qf * jnp.exp(jnp.minimum(b - g, 0.0))).astype(BF16)
                k_l = (kf * jnp.exp(jnp.minimum(g - b, 0.0))).astype(BF16)
                same_parent = (row >> (lvl + 1)) == (col >> (lvl + 1))
                if reverse:
                    pair = same_parent & ((col >> lvl) == (row >> lvl) + 1)
                else:
                    pair = same_parent & ((row >> lvl) == (col >> lvl) + 1)
                att = att + jnp.where(pair, _dot_nt(q_l, k_l), 0.0)
                lvl += 1
        o = o + _dot(att.astype(BF16), vb)
        o_ref[0, :, hs] = o.astype(o_ref.dtype)
        st_ref[h] = st * jnp.exp(bl) + _dot_tn(vb, k_out)


def _gla_kernel(qf_ref, lff_ref, vf_ref, qb_ref, lfb_ref, vb_ref, of_ref, ob_ref, sf_ref, sb_ref):
    @pl.when(pl.program_id(1) == 0)
    def _():
        sf_ref[...] = jnp.zeros_like(sf_ref)
        sb_ref[...] = jnp.zeros_like(sb_ref)

    c = qf_ref.shape[1]
    row = _iota2((c, c), 0)
    col = _iota2((c, c), 1)
    b_fw = _sel_rows((row >= col).astype(BF16), lff_ref[0])
    b_bw = _sel_rows((row <= col).astype(BF16), lfb_ref[0])
    total = jnp.minimum(jnp.min(b_fw[c - 1:c, :]), jnp.min(b_bw[0:1, :]))
    safe = total >= -GLA_SAFE_LOG_DECAY

    for flag, pred in ((True, safe), (False, jnp.logical_not(safe))):
        @pl.when(pred)
        def _(flag=flag):
            _gla_step(qf_ref, lff_ref, vf_ref, of_ref, sf_ref, reverse=False, safe=flag, b_all=b_fw)
            _gla_step(qb_ref, lfb_ref, vb_ref, ob_ref, sb_ref, reverse=True, safe=flag, b_all=b_bw)


def _gla(q, lf_fw, lf_bw, v, chunk):
    bsz, seq, d = q.shape
    nc = seq // chunk
    n_heads = d // HG_HEAD_DIM
    fw = pl.BlockSpec((1, chunk, d), lambda b, c: (b, c, 0))
    bw = pl.BlockSpec((1, chunk, d), lambda b, c: (b, nc - 1 - c, 0))
    state = pltpu.VMEM((n_heads, HG_HEAD_DIM, HG_HEAD_DIM), F32)
    return pl.pallas_call(
        _gla_kernel,
        grid=(bsz, nc),
        in_specs=[fw, fw, fw, bw, bw, bw],
        out_specs=[fw, bw],
        out_shape=[jax.ShapeDtypeStruct((bsz, seq, d), BF16)] * 2,
        scratch_shapes=[state, state],
        compiler_params=_cparams("parallel", "arbitrary"),
        name="hgrn_gla",
    )(q, lf_fw, v, q, lf_bw, v)


def _hgrn_out_kernel(of_ref, ob_ref, g_ref, h_ref, nw_ref, w_ref, out_ref):
    d = h_ref.shape[1]
    nw = nw_ref[...]
    parts = []
    for c0 in range(0, d, HG_HEAD_DIM):
        hs = slice(c0, c0 + HG_HEAD_DIM)
        o = of_ref[:, hs].astype(F32) + ob_ref[:, hs].astype(F32)
        y = _rmsnorm(o, nw) * g_ref[:, hs].astype(F32)
        parts.append(y.astype(BF16))
    y = jnp.concatenate(parts, axis=1)
    out_ref[...] = h_ref[...] + _dot(y, w_ref[...])


def _hgrn_out(o_fw, o_bw, g, h2, norm_w, w_out_bf, tm):
    t, d = h2.shape
    tile = pl.BlockSpec((tm, d), lambda i: (i, 0))
    return pl.pallas_call(
        _hgrn_out_kernel,
        grid=(t // tm,),
        in_specs=[tile, tile, tile, tile, _const_spec((1, HG_HEAD_DIM)), _const_spec(w_out_bf.shape)],
        out_specs=tile,
        out_shape=jax.ShapeDtypeStruct((t, d), F32),
        compiler_params=_cparams("parallel"),
        name="hgrn_out",
    )(o_fw, o_bw, g, h2, norm_w.reshape(1, HG_HEAD_DIM), w_out_bf)


def _halo_rows(hp_ref, hn_ref, nw):
    up = _rmsnorm(hp_ref[0], nw)
    un = _rmsnorm(hn_ref[0], nw)
    return jnp.concatenate([up, un], axis=0).astype(BF16)


def _halo_inside():
    i = pl.program_id(1)
    r = _iota2((2 * SUBLANES, 1), 0)
    has_prev = (i > 0).astype(F32)
    has_next = (i < pl.num_programs(1) - 1).astype(F32)
    return jnp.where(r < SUBLANES, has_prev, has_next) > 0.5


def _shifted(e_ref, x, xh, halo, n_rows):
    e_ref[0:SUBLANES, :] = xh[0:SUBLANES]
    e_ref[SUBLANES:SUBLANES + n_rows, :] = x
    e_ref[SUBLANES + n_rows:, :] = xh[SUBLANES:]
    return [x if j == halo else e_ref[pl.ds(SUBLANES - halo + j, n_rows), :] for j in range(2 * halo + 1)]


def _ffn_kernel(h_ref, hp_ref, hn_ref, nw_ref, wg_ref, wv_ref, cw_ref, cb_ref, wo_ref, fw_ref,
                out_ref, a_ref, e_ref, *, col_chunk, final_norm):
    tm = h_ref.shape[1]
    d_ff = wg_ref.shape[1]
    nw = nw_ref[...]
    hx = h_ref[0]
    u = _rmsnorm(hx, nw).astype(BF16)
    uh = _halo_rows(hp_ref, hn_ref, nw)
    inside = _halo_inside()
    for c0 in range(0, d_ff, col_chunk):
        cols = slice(c0, c0 + col_chunk)
        gate = _dot(u, wg_ref[:, cols])
        gh = jnp.where(inside, _dot(uh, wg_ref[:, cols]), 0.0)
        taps = _shifted(e_ref, gate, gh, FFN_CONV_HALO, tm)
        conv = cb_ref[:, cols]
        for j, tap in enumerate(taps):
            conv = conv + cw_ref[j:j + 1, cols] * tap
        val = _dot(u, wv_ref[:, cols])
        a_ref[:, cols] = (_silu(conv) * val).astype(BF16)
    y = hx + _dot(a_ref[...], wo_ref[...])
    if final_norm:
        y = _rmsnorm(y, fw_ref[...])
    out_ref[0] = y


def _halo_specs(tm, d, seq):
    nb = tm // SUBLANES
    n_blocks = seq // SUBLANES
    prev = pl.BlockSpec((1, SUBLANES, d), lambda b, i: (b, jnp.maximum(i * nb - 1, 0), 0))
    nxt = pl.BlockSpec((1, SUBLANES, d), lambda b, i: (b, jnp.minimum((i + 1) * nb, n_blocks - 1), 0))
    return prev, nxt


def _ffn(h3, norm_w, wg_bf, wv_bf, conv_w, conv_b, wo_bf, final_w, tm, final_norm):
    bsz, seq, d = h3.shape
    d_ff = wg_bf.shape[1]
    col_chunk = 256
    tile = pl.BlockSpec((1, tm, d), lambda b, i: (b, i, 0))
    prev, nxt = _halo_specs(tm, d, seq)
    return pl.pallas_call(
        functools.partial(_ffn_kernel, col_chunk=col_chunk, final_norm=final_norm),
        grid=(bsz, seq // tm),
        in_specs=[tile, prev, nxt, _const_spec((1, d)), _const_spec(wg_bf.shape), _const_spec(wv_bf.shape),
                  _const_spec(conv_w.shape), _const_spec((1, d_ff)), _const_spec(wo_bf.shape),
                  _const_spec((1, d))],
        out_specs=tile,
        out_shape=jax.ShapeDtypeStruct(h3.shape, F32),
        scratch_shapes=[pltpu.VMEM((tm, d_ff), BF16), pltpu.VMEM((tm + 2 * SUBLANES, col_chunk), F32)],
        compiler_params=_cparams("parallel", "parallel"),
        name="convglu_final" if final_norm else "convglu",
    )(h3, h3, h3, norm_w.reshape(1, d), wg_bf, wv_bf, conv_w, conv_b.reshape(1, d_ff), wo_bf,
      final_w.reshape(1, d))


def _ssm_proj_kernel(h_ref, hp_ref, hn_ref, nw_ref, wz_ref, wx_ref, wdt_ref, cw_ref, cb_ref, dtb_ref,
                     z_ref, xs_ref, bm_ref, cm_ref, dt_ref, e_ref, *, col_chunk):
    tm = h_ref.shape[1]
    d_inner = z_ref.shape[2]
    gn = bm_ref.shape[2]
    nw = nw_ref[...]
    u = _rmsnorm(h_ref[0], nw).astype(BF16)
    uh = _halo_rows(hp_ref, hn_ref, nw)
    inside = _halo_inside()
    for c0 in range(0, d_inner, col_chunk):
        cols = slice(c0, c0 + col_chunk)
        z_ref[0, :, cols] = _silu(_dot(u, wz_ref[:, cols])).astype(BF16)
    for c0 in range(0, d_inner + 2 * gn, col_chunk):
        cols = slice(c0, c0 + col_chunk)
        x = _dot(u, wx_ref[:, cols])
        xh = jnp.where(inside, _dot(uh, wx_ref[:, cols]), 0.0)
        taps = _shifted(e_ref, x, xh, SSM_CONV_HALO, tm)
        conv = cb_ref[:, cols]
        for j, tap in enumerate(taps):
            conv = conv + cw_ref[j:j + 1, cols] * tap
        act = _silu(conv).astype(BF16)
        if c0 < d_inner:
            xs_ref[0, :, cols] = act
        elif c0 < d_inner + gn:
            bm_ref[0, :, c0 - d_inner: c0 - d_inner + col_chunk] = act
        else:
            cm_ref[0, :, c0 - d_inner - gn: c0 - d_inner - gn + col_chunk] = act
    dt_ref[0] = _softplus(_dot(u, wdt_ref[...]) + dtb_ref[...])


def _ssm_proj(h3, norm_w, wz_bf, wx_bf, wdt_bf, conv_w, conv_b, dt_bias_row, tm):
    bsz, seq, d = h3.shape
    d_inner = wz_bf.shape[1]
    conv_dim = wx_bf.shape[1]
    gn = (conv_dim - d_inner) // 2
    col_chunk = 512
    tile = pl.BlockSpec((1, tm, d), lambda b, i: (b, i, 0))
    prev, nxt = _halo_specs(tm, d, seq)
    out_tile = lambda w: pl.BlockSpec((1, tm, w), lambda b, i: (b, i, 0))
    tok = lambda w, dt: jax.ShapeDtypeStruct((bsz, seq, w), dt)
    return pl.pallas_call(
        functools.partial(_ssm_proj_kernel, col_chunk=col_chunk),
        grid=(bsz, seq // tm),
        in_specs=[tile, prev, nxt, _const_spec((1, d)), _const_spec(wz_bf.shape), _const_spec(wx_bf.shape),
                  _const_spec(wdt_bf.shape), _const_spec(conv_w.shape), _const_spec((1, conv_dim)),
                  _const_spec((1, LANES))],
        out_specs=[out_tile(d_inner), out_tile(d_inner), out_tile(gn), out_tile(gn), out_tile(LANES)],
        out_shape=[tok(d_inner, BF16), tok(d_inner, BF16), tok(gn, BF16), tok(gn, BF16), tok(LANES, F32)],
        scratch_shapes=[pltpu.VMEM((tm + 2 * SUBLANES, col_chunk), F32)],
        compiler_params=_cparams("parallel", "parallel"),
        name="ssm_proj",
    )(h3, h3, h3, norm_w.reshape(1, d), wz_bf, wx_bf, wdt_bf, conv_w, conv_b.reshape(1, conv_dim), dt_bias_row)


def _ssd_step(xs_ref, bm_ref, cm_ref, dt_ref, alog_ref, y_ref, st_ref, *, reverse):
    c = xs_ref.shape[1]
    n_groups = st_ref.shape[0]
    n_state = st_ref.shape[1]
    gw = st_ref.shape[2]
    pairs = gw // LANES
    heads = n_groups * (gw // SSM_HEAD_DIM)
    h0 = heads if reverse else 0
    row = _iota2((c, c), 0)
    col = _iota2((c, c), 1)
    lane = _iota2((c, LANES), 1)
    keep = (row <= col) if reverse else (row >= col)
    last = 0 if reverse else c - 1

    dt = dt_ref[0]
    la = dt * (-jnp.exp(alog_ref[...]))
    tri = keep.astype(BF16)
    a_col = _sel_rows(tri, la)
    a_row = _sel_cols(la.T, tri.T)
    dt_row = dt.T
    w_row = dt_row * jnp.exp(a_row[:, last:last + 1] - a_row)

    for g in range(n_groups):
        bg = bm_ref[0, :, g * n_state:(g + 1) * n_state]
        cg = cm_ref[0, :, g * n_state:(g + 1) * n_state]
        bt = bg.astype(F32).T
        cb = _dot(cg, bt.astype(BF16))
        for p in range(pairs):
            ha = h0 + g * 2 * pairs + 2 * p
            lanes = slice(g * gw + p * LANES, g * gw + (p + 1) * LANES)
            xt = xs_ref[0, :, lanes]
            zero = jnp.zeros_like(xt)
            x_bd = jnp.concatenate([jnp.where(lane < SSM_HEAD_DIM, xt, zero),
                                    jnp.where(lane >= SSM_HEAD_DIM, xt, zero)], axis=0)
            ws, bws, bcs = [], [], []
            for hh in (ha, ha + 1):
                bc = jnp.broadcast_to(a_col[:, hh:hh + 1], (c, c))
                decay = jnp.exp(jnp.where(keep, bc - a_row[hh:hh + 1, :], -jnp.inf))
                ws.append((cb * decay * dt_row[hh:hh + 1, :]).astype(BF16))
                bws.append((bt * w_row[hh:hh + 1, :]).astype(BF16))
                bcs.append(bc)
            st = st_ref[g, :, p * LANES:(p + 1) * LANES]
            e_t = jnp.exp(jnp.where(lane < SSM_HEAD_DIM, bcs[0], bcs[1]))
            y = _dot(jnp.concatenate(ws, axis=1), x_bd) + _dot(cg, st.astype(BF16)) * e_t
            y_ref[0, :, lanes] = y.astype(y_ref.dtype)
            st_ref[g, :, p * LANES:(p + 1) * LANES] = (
                st * e_t[last:last + 1, :] + _dot(jnp.concatenate(bws, axis=1), x_bd))


def _ssd_kernel(xf_ref, bf_ref, cf_ref, dtf_ref, xb_ref, bb_ref, cb_ref, dtb_ref, alog_ref,
                yf_ref, yb_ref, sf_ref, sb_ref):
    @pl.when(pl.program_id(1) == 0)
    def _():
        sf_ref[...] = jnp.zeros_like(sf_ref)
        sb_ref[...] = jnp.zeros_like(sb_ref)

    _ssd_step(xf_ref, bf_ref, cf_ref, dtf_ref, alog_ref, yf_ref, sf_ref, reverse=False)
    _ssd_step(xb_ref, bb_ref, cb_ref, dtb_ref, alog_ref, yb_ref, sb_ref, reverse=True)


def _ssd(xs, bm, cm, dt, alog_row, chunk):
    bsz, seq, d_inner = xs.shape
    gn = bm.shape[2]
    nc = seq // chunk
    n_state = gn // SSM_GROUPS
    fw = lambda w: pl.BlockSpec((1, chunk, w), lambda b, c: (b, c, 0))
    bw = lambda w: pl.BlockSpec((1, chunk, w), lambda b, c: (b, nc - 1 - c, 0))
    state = pltpu.VMEM((SSM_GROUPS, n_state, d_inner // SSM_GROUPS), F32)
    return pl.pallas_call(
        _ssd_kernel,
        grid=(bsz, nc),
        in_specs=[fw(d_inner), fw(gn), fw(gn), fw(LANES), bw(d_inner), bw(gn), bw(gn), bw(LANES),
                  _const_spec((1, LANES))],
        out_specs=[fw(d_inner), bw(d_inner)],
        out_shape=[jax.ShapeDtypeStruct(xs.shape, BF16)] * 2,
        scratch_shapes=[state, state],
        compiler_params=_cparams("parallel", "arbitrary"),
        name="ssm_ssd",
    )(xs, bm, cm, dt, xs, bm, cm, dt, alog_row)


def _ssm_out_kernel(yf_ref, yb_ref, xs_ref, z_ref, h_ref, dsk_ref, nw_ref, w_ref, out_ref, *, group_w):
    d_inner = z_ref.shape[1]
    parts = []
    for c0 in range(0, d_inner, group_w):
        cs = slice(c0, c0 + group_w)
        y = (yf_ref[:, cs].astype(F32) + yb_ref[:, cs].astype(F32)
             + xs_ref[:, cs].astype(F32) * dsk_ref[:, cs]) * z_ref[:, cs].astype(F32)
        parts.append(_rmsnorm(y, nw_ref[:, cs]).astype(BF16))
    y = jnp.concatenate(parts, axis=1)
    out_ref[...] = h_ref[...] + _dot(y, w_ref[...])


def _ssm_out(y_fw, y_bw, xs, z, h2, d_skip_row, norm_w, w_out_bf, tm):
    t, d = h2.shape
    d_inner = z.shape[1]
    wide = pl.BlockSpec((tm, d_inner), lambda i: (i, 0))
    tile = pl.BlockSpec((tm, d), lambda i: (i, 0))
    return pl.pallas_call(
        functools.partial(_ssm_out_kernel, group_w=d_inner // SSM_GROUPS),
        grid=(t // tm,),
        in_specs=[wide, wide, wide, wide, tile, _const_spec((1, d_inner)), _const_spec((1, d_inner)),
                  _const_spec(w_out_bf.shape)],
        out_specs=tile,
        out_shape=jax.ShapeDtypeStruct((t, d), F32),
        compiler_params=_cparams("parallel"),
        name="ssm_out",
    )(y_fw, y_bw, xs, z, h2, d_skip_row, norm_w.reshape(1, d_inner), w_out_bf)


def kernel(x, norm1_w, norm2_w, a_w_in, a_lb_logits, a_norm_w, a_w_out, b_w_in, b_conv_w, b_conv_b,
           b_dt_bias, b_a_log, b_d_skip, b_norm_w, b_w_out, ffn_w_in, ffn_conv_w, ffn_conv_b, ffn_w_out,
           final_norm_w):
    bsz, seq, d = x.shape
    depth = norm1_w.shape[0]
    t = bsz * seq
    tm = min(512, seq)
    d_ff = ffn_w_out.shape[1]
    d_inner = b_norm_w.shape[1]
    n_ssm_heads = b_dt_bias.shape[2]
    assert seq % tm == 0 and seq % GLA_CHUNK == 0 and seq % SSD_CHUNK == 0
    assert 2 * n_ssm_heads <= LANES and d_inner == n_ssm_heads * SSM_HEAD_DIM

    h = x
    for i in range(depth):
        j = i // 2
        if i % 2 == 0:
            q, lf_fw, lf_bw, v, g = _hgrn_proj(h.reshape(t, d), norm1_w[i], a_w_in[j].astype(BF16),
                                               a_lb_logits, i, tm)
            r3 = lambda a: a.reshape(bsz, seq, d)
            o_fw, o_bw = _gla(r3(q), r3(lf_fw), r3(lf_bw), r3(v), GLA_CHUNK)
            h = _hgrn_out(o_fw.reshape(t, d), o_bw.reshape(t, d), g, h.reshape(t, d), a_norm_w[j],
                          a_w_out[j].astype(BF16), tm).reshape(bsz, seq, d)
        else:
            w_in = b_w_in[j].astype(BF16)
            conv_dim = b_conv_w.shape[2]
            pad = LANES - 2 * n_ssm_heads
            wdt = jnp.pad(w_in[:, d_inner + conv_dim:], ((0, 0), (0, pad)))
            dt_bias_row = jnp.pad(b_dt_bias[j].reshape(1, -1), ((0, 0), (0, pad)))
            alog_row = jnp.pad(b_a_log[j].reshape(1, -1), ((0, 0), (0, pad)))
            z, xs, bm, cm, dt = _ssm_proj(h, norm1_w[i], w_in[:, :d_inner], w_in[:, d_inner:d_inner + conv_dim],
                                          wdt, b_conv_w[j], b_conv_b[j], dt_bias_row, tm)
            y_fw, y_bw = _ssd(xs, bm, cm, dt, alog_row, SSD_CHUNK)
            d_skip_row = jnp.repeat(b_d_skip[j], SSM_HEAD_DIM).reshape(1, d_inner)
            f2 = lambda a: a.reshape(t, d_inner)
            h = _ssm_out(f2(y_fw), f2(y_bw), f2(xs), f2(z), h.reshape(t, d), d_skip_row, b_norm_w[j],
                         b_w_out[j].astype(BF16), tm).reshape(bsz, seq, d)
        w_ffn = ffn_w_in[i].astype(BF16)
        h = _ffn(h, norm2_w[i], w_ffn[:, :d_ff], w_ffn[:, d_ff:], ffn_conv_w[i], ffn_conv_b[i],
                 ffn_w_out[i].astype(BF16), final_norm_w, tm, final_norm=(i == depth - 1))
    return h
```

```python
import functools

import jax
import jax.numpy as jnp
from jax import lax
from jax.experimental import pallas as pl
from jax.experimental.pallas import tpu as pltpu

F32 = jnp.float32
BF16 = jnp.bfloat16

EPS = 1e-6
HG_HEAD_DIM = 128
SSM_HEAD_DIM = 64
SSM_GROUPS = 8
SSM_CONV_HALO = 2
FFN_CONV_HALO = 1
SUBLANES = 8
LANES = 128
GLA_CHUNK = 128
SSD_CHUNK = 128
LOG2E = 1.4426950408889634
GLA_SAFE_LOG_DECAY = 75.0
VMEM_LIMIT_BYTES = 56 * 1024 * 1024


def _cparams(*sem):
    return pltpu.CompilerParams(dimension_semantics=sem, vmem_limit_bytes=VMEM_LIMIT_BYTES)


def _const_spec(shape, index=None):
    index = (0,) * len(shape) if index is None else index
    return pl.BlockSpec(shape, lambda *_: index, pipeline_mode=pl.Buffered(1))


def _rmsnorm(x, w):
    return x * lax.rsqrt(jnp.mean(x * x, axis=-1, keepdims=True) + EPS) * w


def _sigmoid(x):
    return 0.5 * jnp.tanh(0.5 * x) + 0.5


def _silu(x):
    h = 0.5 * x
    return h * jnp.tanh(h) + h


def _softplus(x):
    return jnp.maximum(x, 0.0) + jnp.log(1.0 + jnp.exp(-jnp.abs(x)))


def _dot(a, b):
    return jnp.dot(a, b, preferred_element_type=F32)


def _dot_nt(a, b):
    return lax.dot_general(a, b, (((1,), (1,)), ((), ())), preferred_element_type=F32)


def _dot_tn(a, b):
    return lax.dot_general(a, b, (((0,), (0,)), ((), ())), preferred_element_type=F32)


def _split3(x):
    hi = x.astype(BF16)
    r1 = x - hi.astype(F32)
    mid = r1.astype(BF16)
    lo = (r1 - mid.astype(F32)).astype(BF16)
    return hi, mid, lo


def _split2(x):
    hi = x.astype(BF16)
    return hi, (x - hi.astype(F32)).astype(BF16)


def _sel_rows(m01, x):
    hi, mid, lo = _split3(x)
    return _dot(m01, hi) + _dot(m01, mid) + _dot(m01, lo)


def _sel_cols(x, m01):
    hi, mid, lo = _split3(x)
    return _dot(hi, m01) + _dot(mid, m01) + _dot(lo, m01)


def _iota2(shape, dim):
    return lax.broadcasted_iota(jnp.int32, shape, dim)


def _hgrn_proj_kernel(h_ref, nw_ref, w_ref, lbl_ref, q_ref, lhf_ref, lmf_ref, kf_ref, lhb_ref, lmb_ref, kb_ref,
                      v_ref, g_ref, *, layer, col_chunk):
    d = h_ref.shape[1]
    u = _rmsnorm(h_ref[...], nw_ref[...]).astype(BF16)
    lg = lbl_ref[...]
    n_rows = lg.shape[0]
    mx = lg[0:1]
    for r in range(1, n_rows):
        mx = jnp.maximum(mx, lg[r:r + 1])
    es = [jnp.exp(lg[r:r + 1] - mx) for r in range(n_rows)]
    tot = es[0]
    for r in range(1, n_rows):
        tot = tot + es[r]
    part = es[0]
    for r in range(1, layer + 1):
        part = part + es[r]
    lb = part / tot

    def gate(fr, cols, lh_ref, lm_ref, k_ref):
        lbc = lb[:, cols]
        f = lbc + (1.0 - lbc) * _sigmoid(fr)
        hi, mid = _split2(jnp.log(f) * LOG2E)
        lh_ref[:, cols] = hi
        lm_ref[:, cols] = mid
        k_ref[:, cols] = (1.0 - f).astype(BF16)

    for sec in range(5):
        for c0 in range(0, d, col_chunk):
            cols = slice(c0, c0 + col_chunk)
            p = _dot(u, w_ref[:, sec * d + c0: sec * d + c0 + col_chunk])
            if sec == 0:
                q_ref[:, cols] = _silu(p).astype(BF16)
            elif sec == 1:
                gate(p, cols, lhf_ref, lmf_ref, kf_ref)
            elif sec == 2:
                gate(p, cols, lhb_ref, lmb_ref, kb_ref)
            elif sec == 3:
                v_ref[:, cols] = p.astype(BF16)
            else:
                g_ref[:, cols] = _silu(p).astype(BF16)


def _hgrn_proj(h2, norm_w, w_in_bf, lb_logits, layer, tm):
    t, d = h2.shape
    n_out = 9
    tile = pl.BlockSpec((tm, d), lambda i: (i, 0))
    return pl.pallas_call(
        functools.partial(_hgrn_proj_kernel, layer=layer, col_chunk=min(512, d)),
        grid=(t // tm,),
        in_specs=[tile, _const_spec((1, d)), _const_spec(w_in_bf.shape), _const_spec(lb_logits.shape)],
        out_specs=[tile] * n_out,
        out_shape=[jax.ShapeDtypeStruct((t, d), BF16)] * n_out,
        compiler_params=_cparams("parallel"),
        name="hgrn_proj",
    )(h2, norm_w.reshape(1, d), w_in_bf, lb_logits)


def _gla_step(q_ref, k_ref, v_ref, o_ref, st_ref, *, reverse, safe, b_all):
    c = q_ref.shape[1]
    n_heads = st_ref.shape[0]
    hd = st_ref.shape[2]
    row = _iota2((c, c), 0)
    col = _iota2((c, c), 1)
    last = 0 if reverse else c - 1
    mid = c // 2 if reverse else c // 2 - 1
    for h in range(n_heads):
        hs = slice(h * hd, (h + 1) * hd)
        b = b_all[:, hs]
        bl = b[last:last + 1, :]
        vb = v_ref[0, :, hs]
        st = st_ref[h]
        if safe:
            bm = b[mid:mid + 1, :]
            dm = b - bm
            q_hat = q_ref[0, :, hs] * jnp.exp2(dm).astype(BF16)
            k_hat = k_ref[0, :, hs] * jnp.exp2(-dm).astype(BF16)
            keep = (row <= col) if reverse else (row >= col)
            att = jnp.where(keep, _dot_nt(q_hat, k_hat), 0.0)
            o = _dot_nt(q_hat, (st * jnp.exp2(bm)).astype(BF16)) + _dot(att.astype(BF16), vb)
            kv = _dot_tn(vb, k_hat) * jnp.exp2(bl - bm)
        else:
            qf = q_ref[0, :, hs].astype(F32)
            kf = k_ref[0, :, hs].astype(F32)
            att = jnp.where(row == col, _dot_nt(q_ref[0, :, hs], k_ref[0, :, hs]), 0.0)
            lvl = 0
            while (1 << lvl) < c:
                bs = 1 << lvl
                edge = (row >> (lvl + 1)) * (2 * bs) + (bs if reverse else bs - 1)
                g = _sel_rows((col == edge).astype(BF16), b)
                q_l = (qf * jnp.exp2(jnp.minimum(b - g, 0.0))).astype(BF16)
                k_l = (kf * jnp.exp2(jnp.minimum(g - b, 0.0))).astype(BF16)
                same_parent = (row >> (lvl + 1)) == (col >> (lvl + 1))
                if reverse:
                    pair = same_parent & ((col >> lvl) == (row >> lvl) + 1)
                else:
                    pair = same_parent & ((row >> lvl) == (col >> lvl) + 1)
                att = att + jnp.where(pair, _dot_nt(q_l, k_l), 0.0)
                lvl += 1
            q_in = (qf * jnp.exp2(b)).astype(BF16)
            o = _dot_nt(q_in, st.astype(BF16)) + _dot(att.astype(BF16), vb)
            kv = _dot_tn(vb, (kf * jnp.exp2(bl - b)).astype(BF16))
        o_ref[0, :, hs] = o.astype(o_ref.dtype)
        st_ref[h] = st * jnp.exp2(bl) + kv


def _gla_kernel(qf_ref, lhf_ref, lmf_ref, kf_ref, vf_ref, qb_ref, lhb_ref, lmb_ref, kb_ref, vb_ref,
                of_ref, ob_ref, sf_ref, sb_ref):
    @pl.when(pl.program_id(1) == 0)
    def _():
        sf_ref[...] = jnp.zeros_like(sf_ref)
        sb_ref[...] = jnp.zeros_like(sb_ref)

    c = qf_ref.shape[1]
    row = _iota2((c, c), 0)
    col = _iota2((c, c), 1)
    tri_fw = (row >= col).astype(BF16)
    tri_bw = (row <= col).astype(BF16)
    b_fw = _dot(tri_fw, lhf_ref[0]) + _dot(tri_fw, lmf_ref[0])
    b_bw = _dot(tri_bw, lhb_ref[0]) + _dot(tri_bw, lmb_ref[0])

    def span(b, mid, last):
        bm = b[mid:mid + 1, :]
        return jnp.min(jnp.minimum(bm, b[last:last + 1, :] - bm))

    worst = jnp.minimum(span(b_fw, c // 2 - 1, c - 1), span(b_bw, c // 2, 0))
    safe = worst >= -GLA_SAFE_LOG_DECAY * LOG2E

    for flag, pred in ((True, safe), (False, jnp.logical_not(safe))):
        @pl.when(pred)
        def _(flag=flag):
            _gla_step(qf_ref, kf_ref, vf_ref, of_ref, sf_ref, reverse=False, safe=flag, b_all=b_fw)
            _gla_step(qb_ref, kb_ref, vb_ref, ob_ref, sb_ref, reverse=True, safe=flag, b_all=b_bw)


def _gla(q, lh_fw, lm_fw, k_fw, lh_bw, lm_bw, k_bw, v, chunk):
    bsz, seq, d = q.shape
    nc = seq // chunk
    n_heads = d // HG_HEAD_DIM
    fw = pl.BlockSpec((1, chunk, d), lambda b, c: (b, c, 0))
    bw = pl.BlockSpec((1, chunk, d), lambda b, c: (b, nc - 1 - c, 0))
    state = pltpu.VMEM((n_heads, HG_HEAD_DIM, HG_HEAD_DIM), F32)
    return pl.pallas_call(
        _gla_kernel,
        grid=(bsz, nc),
        in_specs=[fw] * 5 + [bw] * 5,
        out_specs=[fw, bw],
        out_shape=[jax.ShapeDtypeStruct((bsz, seq, d), BF16)] * 2,
        scratch_shapes=[state, state],
        compiler_params=_cparams("parallel", "arbitrary"),
        name="hgrn_gla",
    )(q, lh_fw, lm_fw, k_fw, v, q, lh_bw, lm_bw, k_bw, v)


def _hgrn_out_kernel(of_ref, ob_ref, g_ref, h_ref, nw_ref, w_ref, out_ref):
    d = h_ref.shape[1]
    nw = nw_ref[...]
    parts = []
    for c0 in range(0, d, HG_HEAD_DIM):
        hs = slice(c0, c0 + HG_HEAD_DIM)
        o = of_ref[:, hs].astype(F32) + ob_ref[:, hs].astype(F32)
        y = _rmsnorm(o, nw) * g_ref[:, hs].astype(F32)
        parts.append(y.astype(BF16))
    y = jnp.concatenate(parts, axis=1)
    out_ref[...] = h_ref[...] + _dot(y, w_ref[...])


def _hgrn_out(o_fw, o_bw, g, h2, norm_w, w_out_bf, tm):
    t, d = h2.shape
    tile = pl.BlockSpec((tm, d), lambda i: (i, 0))
    return pl.pallas_call(
        _hgrn_out_kernel,
        grid=(t // tm,),
        in_specs=[tile, tile, tile, tile, _const_spec((1, HG_HEAD_DIM)), _const_spec(w_out_bf.shape)],
        out_specs=tile,
        out_shape=jax.ShapeDtypeStruct((t, d), F32),
        compiler_params=_cparams("parallel"),
        name="hgrn_out",
    )(o_fw, o_bw, g, h2, norm_w.reshape(1, HG_HEAD_DIM), w_out_bf)


def _halo_rows(hp_ref, hn_ref, nw):
    up = _rmsnorm(hp_ref[0], nw)
    un = _rmsnorm(hn_ref[0], nw)
    return jnp.concatenate([up, un], axis=0).astype(BF16)


def _halo_inside():
    i = pl.program_id(1)
    r = _iota2((2 * SUBLANES, 1), 0)
    has_prev = (i > 0).astype(F32)
    has_next = (i < pl.num_programs(1) - 1).astype(F32)
    return jnp.where(r < SUBLANES, has_prev, has_next) > 0.5


def _shifted(e_ref, x, xh, halo, n_rows):
    e_ref[0:SUBLANES, :] = xh[0:SUBLANES]
    e_ref[SUBLANES:SUBLANES + n_rows, :] = x
    e_ref[SUBLANES + n_rows:, :] = xh[SUBLANES:]
    return [x if j == halo else e_ref[pl.ds(SUBLANES - halo + j, n_rows), :] for j in range(2 * halo + 1)]


def _ffn_kernel(h_ref, hp_ref, hn_ref, nw_ref, wg_ref, wv_ref, cw_ref, cb_ref, wo_ref, fw_ref,
                out_ref, a_ref, e_ref, *, col_chunk, final_norm):
    tm = h_ref.shape[1]
    d_ff = wg_ref.shape[1]
    nw = nw_ref[...]
    hx = h_ref[0]
    u = _rmsnorm(hx, nw).astype(BF16)
    uh = _halo_rows(hp_ref, hn_ref, nw)
    inside = _halo_inside()
    for c0 in range(0, d_ff, col_chunk):
        cols = slice(c0, c0 + col_chunk)
        gate = _dot(u, wg_ref[:, cols])
        gh = jnp.where(inside, _dot(uh, wg_ref[:, cols]), 0.0)
        taps = _shifted(e_ref, gate, gh, FFN_CONV_HALO, tm)
        conv = cb_ref[:, cols]
        for j, tap in enumerate(taps):
            conv = conv + cw_ref[j:j + 1, cols] * tap
        val = _dot(u, wv_ref[:, cols])
        a_ref[:, cols] = (_silu(conv) * val).astype(BF16)
    y = hx + _dot(a_ref[...], wo_ref[...])
    if final_norm:
        y = _rmsnorm(y, fw_ref[...])
    out_ref[0] = y


def _halo_specs(tm, d, seq):
    nb = tm // SUBLANES
    n_blocks = seq // SUBLANES
    prev = pl.BlockSpec((1, SUBLANES, d), lambda b, i: (b, jnp.maximum(i * nb - 1, 0), 0))
    nxt = pl.BlockSpec((1, SUBLANES, d), lambda b, i: (b, jnp.minimum((i + 1) * nb, n_blocks - 1), 0))
    return prev, nxt


def _ffn(h3, norm_w, w_in_bf, conv_w, conv_b, wo_bf, final_w, tm, final_norm):
    bsz, seq, d = h3.shape
    d_ff = wo_bf.shape[0]
    col_chunk = 256
    tile = pl.BlockSpec((1, tm, d), lambda b, i: (b, i, 0))
    prev, nxt = _halo_specs(tm, d, seq)
    return pl.pallas_call(
        functools.partial(_ffn_kernel, col_chunk=col_chunk, final_norm=final_norm),
        grid=(bsz, seq // tm),
        in_specs=[tile, prev, nxt, _const_spec((1, d)), _const_spec((d, d_ff), (0, 0)), _const_spec((d, d_ff), (0, 1)),
                  _const_spec(conv_w.shape), _const_spec((1, d_ff)), _const_spec(wo_bf.shape),
                  _const_spec((1, d))],
        out_specs=tile,
        out_shape=jax.ShapeDtypeStruct(h3.shape, F32),
        scratch_shapes=[pltpu.VMEM((tm, d_ff), BF16), pltpu.VMEM((tm + 2 * SUBLANES, col_chunk), F32)],
        compiler_params=_cparams("parallel", "parallel"),
        name="convglu_final" if final_norm else "convglu",
    )(h3, h3, h3, norm_w.reshape(1, d), w_in_bf, w_in_bf, conv_w, conv_b.reshape(1, d_ff), wo_bf,
      final_w.reshape(1, d))


def _ssm_proj_kernel(h_ref, hp_ref, hn_ref, nw_ref, wz_ref, wx_ref, wbc_ref, wdt_ref, cw_ref, cb_ref, dtb_ref,
                     z_ref, xs_ref, bm_ref, cm_ref, dt_ref, e_ref, *, col_chunk):
    tm = h_ref.shape[1]
    d_inner = z_ref.shape[2]
    gn = bm_ref.shape[2]
    nw = nw_ref[...]
    u = _rmsnorm(h_ref[0], nw).astype(BF16)
    uh = _halo_rows(hp_ref, hn_ref, nw)
    inside = _halo_inside()
    for c0 in range(0, d_inner, col_chunk):
        cols = slice(c0, c0 + col_chunk)
        z_ref[0, :, cols] = _silu(_dot(u, wz_ref[:, cols])).astype(BF16)
    for c0 in range(0, d_inner + 2 * gn, col_chunk):
        cols = slice(c0, c0 + col_chunk)
        w = wx_ref[:, cols] if c0 < d_inner else wbc_ref[:, c0 - d_inner: c0 - d_inner + col_chunk]
        x = _dot(u, w)
        xh = jnp.where(inside, _dot(uh, w), 0.0)
        taps = _shifted(e_ref, x, xh, SSM_CONV_HALO, tm)
        conv = cb_ref[:, cols]
        for j, tap in enumerate(taps):
            conv = conv + cw_ref[j:j + 1, cols] * tap
        act = _silu(conv).astype(BF16)
        if c0 < d_inner:
            xs_ref[0, :, cols] = act
        elif c0 < d_inner + gn:
            bm_ref[0, :, c0 - d_inner: c0 - d_inner + col_chunk] = act
        else:
            cm_ref[0, :, c0 - d_inner - gn: c0 - d_inner - gn + col_chunk] = act
    dt_ref[0] = _softplus(_dot(u, wdt_ref[...]) + dtb_ref[...])


def _ssm_proj(h3, norm_w, w_in_bf, wdt_bf, conv_w, conv_b, dt_bias_row, d_inner, tm):
    bsz, seq, d = h3.shape
    conv_dim = conv_w.shape[1]
    gn = (conv_dim - d_inner) // 2
    assert 2 * gn == d_inner
    col_chunk = 512
    tile = pl.BlockSpec((1, tm, d), lambda b, i: (b, i, 0))
    prev, nxt = _halo_specs(tm, d, seq)
    out_tile = lambda w: pl.BlockSpec((1, tm, w), lambda b, i: (b, i, 0))
    tok = lambda w, dt: jax.ShapeDtypeStruct((bsz, seq, w), dt)
    return pl.pallas_call(
        functools.partial(_ssm_proj_kernel, col_chunk=col_chunk),
        grid=(bsz, seq // tm),
        in_specs=[tile, prev, nxt, _const_spec((1, d)), _const_spec((d, d_inner), (0, 0)),
                  _const_spec((d, d_inner), (0, 1)), _const_spec((d, d_inner), (0, 2)),
                  _const_spec(wdt_bf.shape), _const_spec(conv_w.shape), _const_spec((1, conv_dim)),
                  _const_spec((1, LANES))],
        out_specs=[out_tile(d_inner), out_tile(d_inner), out_tile(gn), out_tile(gn), out_tile(LANES)],
        out_shape=[tok(d_inner, BF16), tok(d_inner, BF16), tok(gn, BF16), tok(gn, BF16), tok(LANES, F32)],
        scratch_shapes=[pltpu.VMEM((tm + 2 * SUBLANES, col_chunk), F32)],
        compiler_params=_cparams("parallel", "parallel"),
        name="ssm_proj",
    )(h3, h3, h3, norm_w.reshape(1, d), w_in_bf, w_in_bf, w_in_bf, wdt_bf, conv_w, conv_b.reshape(1, conv_dim),
      dt_bias_row)


def _ssd_step(xs_ref, bm_ref, cm_ref, dt_ref, alog_ref, y_ref, st_ref, *, reverse):
    c = xs_ref.shape[1]
    n_groups = st_ref.shape[0]
    n_state = st_ref.shape[1]
    gw = st_ref.shape[2]
    pairs = gw // LANES
    heads = n_groups * (gw // SSM_HEAD_DIM)
    h0 = heads if reverse else 0
    row = _iota2((c, c), 0)
    col = _iota2((c, c), 1)
    lane = _iota2((c, LANES), 1)
    keep = (row <= col) if reverse else (row >= col)
    last = 0 if reverse else c - 1

    dt = dt_ref[0]
    la = dt * (-LOG2E * jnp.exp(alog_ref[...]))
    tri = keep.astype(BF16)
    a_col = _sel_rows(tri, la)
    a_row = _sel_cols(la.T, tri.T)
    a_src = a_row - jnp.log(dt.T) * LOG2E
    w_row = jnp.exp2(a_row[:, last:last + 1] - a_src)

    for g in range(n_groups):
        bg = bm_ref[0, :, g * n_state:(g + 1) * n_state]
        cg = cm_ref[0, :, g * n_state:(g + 1) * n_state]
        bt = bg.astype(F32).T
        cb = _dot(cg, bt.astype(BF16))
        for p in range(pairs):
            ha = h0 + g * 2 * pairs + 2 * p
            lanes = slice(g * gw + p * LANES, g * gw + (p + 1) * LANES)
            xt = xs_ref[0, :, lanes]
            zero = jnp.zeros_like(xt)
            x_bd = jnp.concatenate([jnp.where(lane < SSM_HEAD_DIM, xt, zero),
                                    jnp.where(lane >= SSM_HEAD_DIM, xt, zero)], axis=0)
            ws, bws, bcs = [], [], []
            for hh in (ha, ha + 1):
                bc = jnp.broadcast_to(a_col[:, hh:hh + 1], (c, c))
                decay = jnp.exp2(jnp.where(keep, bc - a_src[hh:hh + 1, :], -jnp.inf))
                ws.append((cb * decay).astype(BF16))
                bws.append((bt * w_row[hh:hh + 1, :]).astype(BF16))
                bcs.append(bc)
            st = st_ref[g, :, p * LANES:(p + 1) * LANES]
            e_t = jnp.exp2(jnp.where(lane < SSM_HEAD_DIM, bcs[0], bcs[1]))
            y = _dot(jnp.concatenate(ws, axis=1), x_bd) + _dot(cg, st.astype(BF16)) * e_t
            y_ref[0, :, lanes] = y.astype(y_ref.dtype)
            st_ref[g, :, p * LANES:(p + 1) * LANES] = (
                st * e_t[last:last + 1, :] + _dot(jnp.concatenate(bws, axis=1), x_bd))


def _ssd_kernel(xf_ref, bf_ref, cf_ref, dtf_ref, xb_ref, bb_ref, cb_ref, dtb_ref, alog_ref,
                yf_ref, yb_ref, sf_ref, sb_ref):
    @pl.when(pl.program_id(1) == 0)
    def _():
        sf_ref[...] = jnp.zeros_like(sf_ref)
        sb_ref[...] = jnp.zeros_like(sb_ref)

    _ssd_step(xf_ref, bf_ref, cf_ref, dtf_ref, alog_ref, yf_ref, sf_ref, reverse=False)
    _ssd_step(xb_ref, bb_ref, cb_ref, dtb_ref, alog_ref, yb_ref, sb_ref, reverse=True)


def _ssd(xs, bm, cm, dt, alog_row, chunk):
    bsz, seq, d_inner = xs.shape
    gn = bm.shape[2]
    nc = seq // chunk
    n_state = gn // SSM_GROUPS
    fw = lambda w: pl.BlockSpec((1, chunk, w), lambda b, c: (b, c, 0))
    bw = lambda w: pl.BlockSpec((1, chunk, w), lambda b, c: (b, nc - 1 - c, 0))
    state = pltpu.VMEM((SSM_GROUPS, n_state, d_inner // SSM_GROUPS), F32)
    return pl.pallas_call(
        _ssd_kernel,
        grid=(bsz, nc),
        in_specs=[fw(d_inner), fw(gn), fw(gn), fw(LANES), bw(d_inner), bw(gn), bw(gn), bw(LANES),
                  _const_spec((1, LANES))],
        out_specs=[fw(d_inner), bw(d_inner)],
        out_shape=[jax.ShapeDtypeStruct(xs.shape, BF16)] * 2,
        scratch_shapes=[state, state],
        compiler_params=_cparams("parallel", "arbitrary"),
        name="ssm_ssd",
    )(xs, bm, cm, dt, xs, bm, cm, dt, alog_row)


def _ssm_out_kernel(yf_ref, yb_ref, xs_ref, z_ref, h_ref, dsk_ref, nw_ref, w_ref, out_ref, *, group_w):
    d_inner = z_ref.shape[1]
    parts = []
    for c0 in range(0, d_inner, group_w):
        cs = slice(c0, c0 + group_w)
        y = (yf_ref[:, cs].astype(F32) + yb_ref[:, cs].astype(F32)
             + xs_ref[:, cs].astype(F32) * dsk_ref[:, cs]) * z_ref[:, cs].astype(F32)
        parts.append(_rmsnorm(y, nw_ref[:, cs]).astype(BF16))
    y = jnp.concatenate(parts, axis=1)
    out_ref[...] = h_ref[...] + _dot(y, w_ref[...])


def _ssm_out(y_fw, y_bw, xs, z, h2, d_skip_row, norm_w, w_out_bf, tm):
    t, d = h2.shape
    d_inner = z.shape[1]
    wide = pl.BlockSpec((tm, d_inner), lambda i: (i, 0))
    tile = pl.BlockSpec((tm, d), lambda i: (i, 0))
    return pl.pallas_call(
        functools.partial(_ssm_out_kernel, group_w=d_inner // SSM_GROUPS),
        grid=(t // tm,),
        in_specs=[wide, wide, wide, wide, tile, _const_spec((1, d_inner)), _const_spec((1, d_inner)),
                  _const_spec(w_out_bf.shape)],
        out_specs=tile,
        out_shape=jax.ShapeDtypeStruct((t, d), F32),
        compiler_params=_cparams("parallel"),
        name="ssm_out",
    )(y_fw, y_bw, xs, z, h2, d_skip_row, norm_w.reshape(1, d_inner), w_out_bf)


def kernel(x, norm1_w, norm2_w, a_w_in, a_lb_logits, a_norm_w, a_w_out, b_w_in, b_conv_w, b_conv_b,
           b_dt_bias, b_a_log, b_d_skip, b_norm_w, b_w_out, ffn_w_in, ffn_conv_w, ffn_conv_b, ffn_w_out,
           final_norm_w):
    bsz, seq, d = x.shape
    depth = norm1_w.shape[0]
    t = bsz * seq
    tm = min(512, seq)
    d_inner = b_norm_w.shape[1]
    n_ssm_heads = b_dt_bias.shape[2]
    assert seq % tm == 0 and seq % GLA_CHUNK == 0 and seq % SSD_CHUNK == 0
    assert 2 * n_ssm_heads <= LANES and d_inner == n_ssm_heads * SSM_HEAD_DIM

    h = x
    for i in range(depth):
        j = i // 2
        if i % 2 == 0:
            q, lh_fw, lm_fw, k_fw, lh_bw, lm_bw, k_bw, v, g = _hgrn_proj(
                h.reshape(t, d), norm1_w[i], a_w_in[j].astype(BF16), a_lb_logits, i, tm)
            r3 = lambda a: a.reshape(bsz, seq, d)
            o_fw, o_bw = _gla(r3(q), r3(lh_fw), r3(lm_fw), r3(k_fw), r3(lh_bw), r3(lm_bw), r3(k_bw), r3(v),
                              GLA_CHUNK)
            h = _hgrn_out(o_fw.reshape(t, d), o_bw.reshape(t, d), g, h.reshape(t, d), a_norm_w[j],
                          a_w_out[j].astype(BF16), tm).reshape(bsz, seq, d)
        else:
            w_in = b_w_in[j].astype(BF16)
            conv_dim = b_conv_w.shape[2]
            pad = LANES - 2 * n_ssm_heads
            wdt = jnp.pad(w_in[:, d_inner + conv_dim:], ((0, 0), (0, pad)))
            dt_bias_row = jnp.pad(b_dt_bias[j].reshape(1, -1), ((0, 0), (0, pad)))
            alog_row = jnp.pad(b_a_log[j].reshape(1, -1), ((0, 0), (0, pad)))
            z, xs, bm, cm, dt = _ssm_proj(h, norm1_w[i], w_in, wdt, b_conv_w[j], b_conv_b[j], dt_bias_row,
                                          d_inner, tm)
            y_fw, y_bw = _ssd(xs, bm, cm, dt, alog_row, SSD_CHUNK)
            d_skip_row = jnp.repeat(b_d_skip[j], SSM_HEAD_DIM).reshape(1, d_inner)
            f2 = lambda a: a.reshape(t, d_inner)
            h = _ssm_out(f2(y_fw), f2(y_bw), f2(xs), f2(z), h.reshape(t, d), d_skip_row, b_norm_w[j],
                         b_w_out[j].astype(BF16), tm).reshape(bsz, seq, d)
        h = _ffn(h, norm2_w[i], ffn_w_in[i].astype(BF16), ffn_conv_w[i], ffn_conv_b[i],
                 ffn_w_out[i].astype(BF16), final_norm_w, tm, final_norm=(i == depth - 1))
    return h
```

```python
import functools

import jax
import jax.numpy as jnp
from jax import lax
from jax.experimental import pallas as pl
from jax.experimental.pallas import tpu as pltpu

F32 = jnp.float32
BF16 = jnp.bfloat16

EPS = 1e-6
HG_HEAD_DIM = 128
SSM_HEAD_DIM = 64
SSM_GROUPS = 8
SSM_CONV_HALO = 2
FFN_CONV_HALO = 1
SUBLANES = 8
LANES = 128
GLA_CHUNK = 128
SSD_CHUNK = 128
SSD_STREAM = SSD_CHUNK // SUBLANES
LOG2E = 1.4426950408889634
GLA_SAFE_LOG_DECAY = 75.0
VMEM_LIMIT_BYTES = 56 * 1024 * 1024


def _cparams(*sem):
    return pltpu.CompilerParams(dimension_semantics=sem, vmem_limit_bytes=VMEM_LIMIT_BYTES)


def _const_spec(shape, index=None):
    index = (0,) * len(shape) if index is None else index
    return pl.BlockSpec(shape, lambda *_: index, pipeline_mode=pl.Buffered(1))


def _rmsnorm(x, w):
    return x * lax.rsqrt(jnp.mean(x * x, axis=-1, keepdims=True) + EPS) * w


def _sigmoid(x):
    return 0.5 * jnp.tanh(0.5 * x) + 0.5


def _silu(x):
    h = 0.5 * x
    return h * jnp.tanh(h) + h


def _softplus(x):
    return jnp.maximum(x, 0.0) + jnp.log(1.0 + jnp.exp(-jnp.abs(x)))


def _dot(a, b):
    return jnp.dot(a, b, preferred_element_type=F32)


def _dot_nt(a, b):
    return lax.dot_general(a, b, (((1,), (1,)), ((), ())), preferred_element_type=F32)


def _dot_tn(a, b):
    return lax.dot_general(a, b, (((0,), (0,)), ((), ())), preferred_element_type=F32)


def _split3(x):
    hi = x.astype(BF16)
    r1 = x - hi.astype(F32)
    mid = r1.astype(BF16)
    lo = (r1 - mid.astype(F32)).astype(BF16)
    return hi, mid, lo


def _split2(x):
    hi = x.astype(BF16)
    return hi, (x - hi.astype(F32)).astype(BF16)


def _sel_rows(m01, x):
    hi, mid, lo = _split3(x)
    return _dot(m01, hi) + _dot(m01, mid) + _dot(m01, lo)


def _sel_cols(x, m01):
    hi, mid, lo = _split3(x)
    return _dot(hi, m01) + _dot(mid, m01) + _dot(lo, m01)


def _iota2(shape, dim):
    return lax.broadcasted_iota(jnp.int32, shape, dim)


def _hgrn_proj_kernel(h_ref, nw_ref, w_ref, lbl_ref, q_ref, lhf_ref, lmf_ref, kf_ref, lhb_ref, lmb_ref, kb_ref,
                      v_ref, g_ref, *, layer, col_chunk):
    d = h_ref.shape[1]
    u = _rmsnorm(h_ref[...], nw_ref[...]).astype(BF16)
    lg = lbl_ref[...]
    n_rows = lg.shape[0]
    mx = lg[0:1]
    for r in range(1, n_rows):
        mx = jnp.maximum(mx, lg[r:r + 1])
    es = [jnp.exp(lg[r:r + 1] - mx) for r in range(n_rows)]
    tot = es[0]
    for r in range(1, n_rows):
        tot = tot + es[r]
    part = es[0]
    for r in range(1, layer + 1):
        part = part + es[r]
    lb = part / tot

    def gate(fr, cols, lh_ref, lm_ref, k_ref):
        lbc = lb[:, cols]
        f = lbc + (1.0 - lbc) * _sigmoid(fr)
        hi, mid = _split2(jnp.log(f) * LOG2E)
        lh_ref[:, cols] = hi
        lm_ref[:, cols] = mid
        k_ref[:, cols] = (1.0 - f).astype(BF16)

    for sec in range(5):
        for c0 in range(0, d, col_chunk):
            cols = slice(c0, c0 + col_chunk)
            p = _dot(u, w_ref[:, sec * d + c0: sec * d + c0 + col_chunk])
            if sec == 0:
                q_ref[:, cols] = _silu(p).astype(BF16)
            elif sec == 1:
                gate(p, cols, lhf_ref, lmf_ref, kf_ref)
            elif sec == 2:
                gate(p, cols, lhb_ref, lmb_ref, kb_ref)
            elif sec == 3:
                v_ref[:, cols] = p.astype(BF16)
            else:
                g_ref[:, cols] = _silu(p).astype(BF16)


def _hgrn_proj(h2, norm_w, w_in_bf, lb_logits, layer, tm):
    t, d = h2.shape
    n_out = 9
    tile = pl.BlockSpec((tm, d), lambda i: (i, 0))
    return pl.pallas_call(
        functools.partial(_hgrn_proj_kernel, layer=layer, col_chunk=min(512, d)),
        grid=(t // tm,),
        in_specs=[tile, _const_spec((1, d)), _const_spec(w_in_bf.shape), _const_spec(lb_logits.shape)],
        out_specs=[tile] * n_out,
        out_shape=[jax.ShapeDtypeStruct((t, d), BF16)] * n_out,
        compiler_params=_cparams("parallel"),
        name="hgrn_proj",
    )(h2, norm_w.reshape(1, d), w_in_bf, lb_logits)


def _gla_step(q_ref, k_ref, v_ref, o_ref, st_ref, *, reverse, safe, b_all):
    c = q_ref.shape[1]
    n_heads = st_ref.shape[0]
    hd = st_ref.shape[2]
    row = _iota2((c, c), 0)
    col = _iota2((c, c), 1)
    last = 0 if reverse else c - 1
    mid = c // 2 if reverse else c // 2 - 1
    for h in range(n_heads):
        hs = slice(h * hd, (h + 1) * hd)
        b = b_all[:, hs]
        bl = b[last:last + 1, :]
        vb = v_ref[0, :, hs]
        st = st_ref[h]
        if safe:
            bm = b[mid:mid + 1, :]
            dm = b - bm
            q_hat = q_ref[0, :, hs] * jnp.exp2(dm).astype(BF16)
            k_hat = k_ref[0, :, hs] * jnp.exp2(-dm).astype(BF16)
            keep = (row <= col) if reverse else (row >= col)
            att = jnp.where(keep, _dot_nt(q_hat, k_hat), 0.0)
            o = _dot_nt(q_hat, (st * jnp.exp2(bm)).astype(BF16)) + _dot(att.astype(BF16), vb)
            kv = _dot_tn(vb, k_hat) * jnp.exp2(bl - bm)
        else:
            qf = q_ref[0, :, hs].astype(F32)
            kf = k_ref[0, :, hs].astype(F32)
            att = jnp.where(row == col, _dot_nt(q_ref[0, :, hs], k_ref[0, :, hs]), 0.0)
            lvl = 0
            while (1 << lvl) < c:
                bs = 1 << lvl
                edge = (row >> (lvl + 1)) * (2 * bs) + (bs if reverse else bs - 1)
                g = _sel_rows((col == edge).astype(BF16), b)
                q_l = (qf * jnp.exp2(jnp.minimum(b - g, 0.0))).astype(BF16)
                k_l = (kf * jnp.exp2(jnp.minimum(g - b, 0.0))).astype(BF16)
                same_parent = (row >> (lvl + 1)) == (col >> (lvl + 1))
                if reverse:
                    pair = same_parent & ((col >> lvl) == (row >> lvl) + 1)
                else:
                    pair = same_parent & ((row >> lvl) == (col >> lvl) + 1)
                att = att + jnp.where(pair, _dot_nt(q_l, k_l), 0.0)
                lvl += 1
            q_in = (qf * jnp.exp2(b)).astype(BF16)
            o = _dot_nt(q_in, st.astype(BF16)) + _dot(att.astype(BF16), vb)
            kv = _dot_tn(vb, (kf * jnp.exp2(bl - b)).astype(BF16))
        o_ref[0, :, hs] = o.astype(o_ref.dtype)
        st_ref[h] = st * jnp.exp2(bl) + kv


def _gla_kernel(qf_ref, lhf_ref, lmf_ref, kf_ref, vf_ref, qb_ref, lhb_ref, lmb_ref, kb_ref, vb_ref,
                of_ref, ob_ref, sf_ref, sb_ref):
    @pl.when(pl.program_id(1) == 0)
    def _():
        sf_ref[...] = jnp.zeros_like(sf_ref)
        sb_ref[...] = jnp.zeros_like(sb_ref)

    c = qf_ref.shape[1]
    row = _iota2((c, c), 0)
    col = _iota2((c, c), 1)
    tri_fw = (row >= col).astype(BF16)
    tri_bw = (row <= col).astype(BF16)
    b_fw = _dot(tri_fw, lhf_ref[0]) + _dot(tri_fw, lmf_ref[0])
    b_bw = _dot(tri_bw, lhb_ref[0]) + _dot(tri_bw, lmb_ref[0])

    def span(b, mid, last):
        bm = b[mid:mid + 1, :]
        return jnp.min(jnp.minimum(bm, b[last:last + 1, :] - bm))

    worst = jnp.minimum(span(b_fw, c // 2 - 1, c - 1), span(b_bw, c // 2, 0))
    safe = worst >= -GLA_SAFE_LOG_DECAY * LOG2E

    for flag, pred in ((True, safe), (False, jnp.logical_not(safe))):
        @pl.when(pred)
        def _(flag=flag):
            _gla_step(qf_ref, kf_ref, vf_ref, of_ref, sf_ref, reverse=False, safe=flag, b_all=b_fw)
            _gla_step(qb_ref, kb_ref, vb_ref, ob_ref, sb_ref, reverse=True, safe=flag, b_all=b_bw)


def _gla(q, lh_fw, lm_fw, k_fw, lh_bw, lm_bw, k_bw, v, chunk):
    bsz, seq, d = q.shape
    nc = seq // chunk
    n_heads = d // HG_HEAD_DIM
    fw = pl.BlockSpec((1, chunk, d), lambda b, c: (b, c, 0))
    bw = pl.BlockSpec((1, chunk, d), lambda b, c: (b, nc - 1 - c, 0))
    state = pltpu.VMEM((n_heads, HG_HEAD_DIM, HG_HEAD_DIM), F32)
    return pl.pallas_call(
        _gla_kernel,
        grid=(bsz, nc),
        in_specs=[fw] * 5 + [bw] * 5,
        out_specs=[fw, bw],
        out_shape=[jax.ShapeDtypeStruct((bsz, seq, d), BF16)] * 2,
        scratch_shapes=[state, state],
        compiler_params=_cparams("parallel", "arbitrary"),
        name="hgrn_gla",
    )(q, lh_fw, lm_fw, k_fw, v, q, lh_bw, lm_bw, k_bw, v)


def _hgrn_out_kernel(of_ref, ob_ref, g_ref, h_ref, nw_ref, w_ref, out_ref):
    d = h_ref.shape[1]
    nw = nw_ref[...]
    parts = []
    for c0 in range(0, d, HG_HEAD_DIM):
        hs = slice(c0, c0 + HG_HEAD_DIM)
        o = of_ref[:, hs].astype(F32) + ob_ref[:, hs].astype(F32)
        y = _rmsnorm(o, nw) * g_ref[:, hs].astype(F32)
        parts.append(y.astype(BF16))
    y = jnp.concatenate(parts, axis=1)
    out_ref[...] = h_ref[...] + _dot(y, w_ref[...])


def _hgrn_out(o_fw, o_bw, g, h2, norm_w, w_out_bf, tm):
    t, d = h2.shape
    tile = pl.BlockSpec((tm, d), lambda i: (i, 0))
    return pl.pallas_call(
        _hgrn_out_kernel,
        grid=(t // tm,),
        in_specs=[tile, tile, tile, tile, _const_spec((1, HG_HEAD_DIM)), _const_spec(w_out_bf.shape)],
        out_specs=tile,
        out_shape=jax.ShapeDtypeStruct((t, d), F32),
        compiler_params=_cparams("parallel"),
        name="hgrn_out",
    )(o_fw, o_bw, g, h2, norm_w.reshape(1, HG_HEAD_DIM), w_out_bf)


def _halo_rows(hp_ref, hn_ref, nw):
    up = _rmsnorm(hp_ref[0], nw)
    un = _rmsnorm(hn_ref[0], nw)
    return jnp.concatenate([up, un], axis=0).astype(BF16)


def _halo_inside():
    i = pl.program_id(1)
    r = _iota2((2 * SUBLANES, 1), 0)
    has_prev = (i > 0).astype(F32)
    has_next = (i < pl.num_programs(1) - 1).astype(F32)
    return jnp.where(r < SUBLANES, has_prev, has_next) > 0.5


def _shifted(e_ref, x, xh, halo, n_rows):
    e_ref[0:SUBLANES, :] = xh[0:SUBLANES]
    e_ref[SUBLANES:SUBLANES + n_rows, :] = x
    e_ref[SUBLANES + n_rows:, :] = xh[SUBLANES:]
    return [x if j == halo else e_ref[pl.ds(SUBLANES - halo + j, n_rows), :] for j in range(2 * halo + 1)]


def _ffn_kernel(h_ref, hp_ref, hn_ref, nw_ref, wg_ref, wv_ref, cw_ref, cb_ref, wo_ref, fw_ref,
                out_ref, a_ref, e_ref, *, col_chunk, final_norm):
    tm = h_ref.shape[1]
    d_ff = wg_ref.shape[1]
    nw = nw_ref[...]
    hx = h_ref[0]
    u = _rmsnorm(hx, nw).astype(BF16)
    uh = _halo_rows(hp_ref, hn_ref, nw)
    inside = _halo_inside()
    for c0 in range(0, d_ff, col_chunk):
        cols = slice(c0, c0 + col_chunk)
        gate = _dot(u, wg_ref[:, cols])
        gh = jnp.where(inside, _dot(uh, wg_ref[:, cols]), 0.0)
        taps = _shifted(e_ref, gate, gh, FFN_CONV_HALO, tm)
        conv = cb_ref[:, cols]
        for j, tap in enumerate(taps):
            conv = conv + cw_ref[j:j + 1, cols] * tap
        val = _dot(u, wv_ref[:, cols])
        a_ref[:, cols] = (_silu(conv) * val).astype(BF16)
    y = hx + _dot(a_ref[...], wo_ref[...])
    if final_norm:
        y = _rmsnorm(y, fw_ref[...])
    out_ref[0] = y


def _halo_specs(tm, d, seq):
    nb = tm // SUBLANES
    n_blocks = seq // SUBLANES
    prev = pl.BlockSpec((1, SUBLANES, d), lambda b, i: (b, jnp.maximum(i * nb - 1, 0), 0))
    nxt = pl.BlockSpec((1, SUBLANES, d), lambda b, i: (b, jnp.minimum((i + 1) * nb, n_blocks - 1), 0))
    return prev, nxt


def _ffn(h3, norm_w, w_in_bf, conv_w, conv_b, wo_bf, final_w, tm, final_norm):
    bsz, seq, d = h3.shape
    d_ff = wo_bf.shape[0]
    col_chunk = 256
    tile = pl.BlockSpec((1, tm, d), lambda b, i: (b, i, 0))
    prev, nxt = _halo_specs(tm, d, seq)
    return pl.pallas_call(
        functools.partial(_ffn_kernel, col_chunk=col_chunk, final_norm=final_norm),
        grid=(bsz, seq // tm),
        in_specs=[tile, prev, nxt, _const_spec((1, d)), _const_spec((d, d_ff), (0, 0)), _const_spec((d, d_ff), (0, 1)),
                  _const_spec(conv_w.shape), _const_spec((1, d_ff)), _const_spec(wo_bf.shape),
                  _const_spec((1, d))],
        out_specs=tile,
        out_shape=jax.ShapeDtypeStruct(h3.shape, F32),
        scratch_shapes=[pltpu.VMEM((tm, d_ff), BF16), pltpu.VMEM((tm + 2 * SUBLANES, col_chunk), F32)],
        compiler_params=_cparams("parallel", "parallel"),
        name="convglu_final" if final_norm else "convglu",
    )(h3, h3, h3, norm_w.reshape(1, d), w_in_bf, w_in_bf, conv_w, conv_b.reshape(1, d_ff), wo_bf,
      final_w.reshape(1, d))


def _chunk_token(idx):
    return (idx & (SUBLANES - 1)) * SSD_STREAM + (idx >> 3)


def _interleave_rows(x, slab_ref):
    tm, d = x.shape
    for s in range(d // LANES):
        for g in range(0, tm, SSD_CHUNK):
            for r in range(SUBLANES):
                slab_ref[s, pl.ds(g + r, SSD_STREAM, stride=SUBLANES), :] = (
                    x[g + r * SSD_STREAM: g + (r + 1) * SSD_STREAM, s * LANES:(s + 1) * LANES])
    return jnp.concatenate([slab_ref[s] for s in range(d // LANES)], axis=1)


def _deinterleave_add(res_ref, slab_ref, out_ref, y):
    tm, d = y.shape
    for s in range(d // LANES):
        slab_ref[s] = y[:, s * LANES:(s + 1) * LANES]
    for s in range(d // LANES):
        for g in range(0, tm, SSD_CHUNK):
            for r in range(SUBLANES):
                rows = slice(g + r * SSD_STREAM, g + (r + 1) * SSD_STREAM)
                lanes = slice(s * LANES, (s + 1) * LANES)
                out_ref[rows, lanes] = res_ref[rows, lanes] + slab_ref[s, pl.ds(g + r, SSD_STREAM, stride=SUBLANES), :]


def _interleaved_taps(x, xh, halo):
    tm, w = x.shape
    nv = SSD_CHUNK // SUBLANES
    n_chunks = tm // SSD_CHUNK
    sub = _iota2((SUBLANES, w), 0)
    tile = lambda g, j: x[g * SSD_CHUNK + j * SUBLANES: g * SSD_CHUNK + (j + 1) * SUBLANES]

    def after(g, j):
        if g + 1 < n_chunks:
            return tile(g + 1, j)
        nxt = xh[SUBLANES:]
        return nxt if j == 0 else pltpu.roll(nxt, SUBLANES - j, 0)

    def before(g, j):
        if g > 0:
            return tile(g - 1, j)
        prv = xh[:SUBLANES]
        return prv if j == nv - 1 else pltpu.roll(prv, nv - 1 - j, 0)

    taps = [[] for _ in range(2 * halo + 1)]
    for g in range(n_chunks):
        lead = [pltpu.roll(jnp.where(sub == SUBLANES - 1, before(g, nv - halo + m), tile(g, nv - halo + m)), 1, 0)
                for m in range(halo)]
        tail = [pltpu.roll(jnp.where(sub == 0, after(g, m), tile(g, m)), SUBLANES - 1, 0) for m in range(halo)]
        ext = jnp.concatenate(lead + [x[g * SSD_CHUNK:(g + 1) * SSD_CHUNK]] + tail, axis=0)
        for j in range(2 * halo + 1):
            taps[j].append(ext[j * SUBLANES: j * SUBLANES + SSD_CHUNK])
    return [jnp.concatenate(t, axis=0) for t in taps]


def _ssm_proj_kernel(h_ref, hp_ref, hn_ref, nw_ref, wz_ref, wx_ref, wbc_ref, wdt_ref, cw_ref, cb_ref, dtb_ref,
                     z_ref, xs_ref, bm_ref, cm_ref, dt_ref, slab_ref, *, col_chunk):
    d_inner = z_ref.shape[2]
    gn = bm_ref.shape[2]
    nw = nw_ref[...]
    u = _rmsnorm(_interleave_rows(h_ref[0], slab_ref), nw).astype(BF16)
    uh = _halo_rows(hp_ref, hn_ref, nw)
    inside = _halo_inside()
    for c0 in range(0, d_inner, col_chunk):
        cols = slice(c0, c0 + col_chunk)
        z_ref[0, :, cols] = _silu(_dot(u, wz_ref[:, cols])).astype(BF16)
    for c0 in range(0, d_inner + 2 * gn, col_chunk):
        cols = slice(c0, c0 + col_chunk)
        w = wx_ref[:, cols] if c0 < d_inner else wbc_ref[:, c0 - d_inner: c0 - d_inner + col_chunk]
        x = _dot(u, w)
        xh = jnp.where(inside, _dot(uh, w), 0.0)
        taps = _interleaved_taps(x, xh, SSM_CONV_HALO)
        conv = cb_ref[:, cols]
        for j, tap in enumerate(taps):
            conv = conv + cw_ref[j:j + 1, cols] * tap
        act = _silu(conv).astype(BF16)
        if c0 < d_inner:
            xs_ref[0, :, cols] = act
        elif c0 < d_inner + gn:
            bm_ref[0, :, c0 - d_inner: c0 - d_inner + col_chunk] = act
        else:
            cm_ref[0, :, c0 - d_inner - gn: c0 - d_inner - gn + col_chunk] = act
    dt_ref[0] = _softplus(_dot(u, wdt_ref[...]) + dtb_ref[...])


def _ssm_proj(h3, norm_w, w_in_bf, wdt_bf, conv_w, conv_b, dt_bias_row, d_inner, tm):
    bsz, seq, d = h3.shape
    conv_dim = conv_w.shape[1]
    gn = (conv_dim - d_inner) // 2
    assert 2 * gn == d_inner
    col_chunk = 512
    tile = pl.BlockSpec((1, tm, d), lambda b, i: (b, i, 0))
    prev, nxt = _halo_specs(tm, d, seq)
    out_tile = lambda w: pl.BlockSpec((1, tm, w), lambda b, i: (b, i, 0))
    tok = lambda w, dt: jax.ShapeDtypeStruct((bsz, seq, w), dt)
    return pl.pallas_call(
        functools.partial(_ssm_proj_kernel, col_chunk=col_chunk),
        grid=(bsz, seq // tm),
        in_specs=[tile, prev, nxt, _const_spec((1, d)), _const_spec((d, d_inner), (0, 0)),
                  _const_spec((d, d_inner), (0, 1)), _const_spec((d, d_inner), (0, 2)),
                  _const_spec(wdt_bf.shape), _const_spec(conv_w.shape), _const_spec((1, conv_dim)),
                  _const_spec((1, LANES))],
        out_specs=[out_tile(d_inner), out_tile(d_inner), out_tile(gn), out_tile(gn), out_tile(LANES)],
        out_shape=[tok(d_inner, BF16), tok(d_inner, BF16), tok(gn, BF16), tok(gn, BF16), tok(LANES, F32)],
        scratch_shapes=[pltpu.VMEM((d // LANES, tm, LANES), F32)],
        compiler_params=_cparams("parallel", "parallel"),
        name="ssm_proj",
    )(h3, h3, h3, norm_w.reshape(1, d), w_in_bf, w_in_bf, w_in_bf, wdt_bf, conv_w, conv_b.reshape(1, conv_dim),
      dt_bias_row)


def _ssd_step(xs_ref, bm_ref, cm_ref, dt_ref, alog_ref, y_ref, st_ref, *, reverse):
    c = xs_ref.shape[1]
    n_groups = st_ref.shape[0]
    n_state = st_ref.shape[1]
    gw = st_ref.shape[2]
    heads = n_groups * (gw // SSM_HEAD_DIM)
    h0 = heads if reverse else 0
    row = _chunk_token(_iota2((c, c), 0))
    col = _chunk_token(_iota2((c, c), 1))
    lane = _iota2((c, LANES), 1)
    keep = (row <= col) if reverse else (row >= col)
    last = 0 if reverse else c - 1

    dt = dt_ref[0]
    la = dt * (-LOG2E * jnp.exp(alog_ref[...]))
    tri = keep.astype(BF16)
    a_col = _sel_rows(tri, la)
    a_row = _sel_cols(la.T, tri.T)
    a_src = a_row - jnp.log(dt.T) * LOG2E
    w_row_bf = jnp.exp2(a_row[:, last:last + 1] - a_src).astype(BF16)

    hpg = gw // SSM_HEAD_DIM
    lane_g = _iota2((c, gw), 1)
    for g in range(n_groups):
        bg = bm_ref[0, :, g * n_state:(g + 1) * n_state]
        cg = cm_ref[0, :, g * n_state:(g + 1) * n_state]
        bt = bg.astype(F32).T.astype(BF16)
        cb = _dot(cg, bt).astype(BF16)
        xg = xs_ref[0, :, g * gw:(g + 1) * gw]
        zero = jnp.zeros_like(xg)
        ws, bws, bcs, x_blocks = [], [], [], []
        for j in range(hpg):
            hh = h0 + g * hpg + j
            bc = jnp.broadcast_to(a_col[:, hh:hh + 1], (c, c))
            decay = jnp.exp2(jnp.where(keep, bc - a_src[hh:hh + 1, :], -jnp.inf))
            ws.append(cb * decay.astype(BF16))
            bws.append(bt * w_row_bf[hh:hh + 1, :])
            bcs.append(bc)
            x_blocks.append(jnp.where((lane_g >= j * SSM_HEAD_DIM) & (lane_g < (j + 1) * SSM_HEAD_DIM), xg, zero))
        lhs = jnp.concatenate([jnp.concatenate(ws, axis=1), jnp.concatenate(bws, axis=1)], axis=0)
        res = _dot(lhs, jnp.concatenate(x_blocks, axis=0))
        st = st_ref[g]
        y_in = _dot(cg, st.astype(BF16))
        e_parts = []
        for p in range(gw // LANES):
            e_parts.append(jnp.exp2(jnp.where(lane < SSM_HEAD_DIM, bcs[2 * p], bcs[2 * p + 1])))
        e_t = jnp.concatenate(e_parts, axis=1)
        y_ref[0, :, g * gw:(g + 1) * gw] = (res[:c] + y_in * e_t).astype(y_ref.dtype)
        st_ref[g] = st * e_t[last:last + 1, :] + res[c:]


def _ssd_kernel(xf_ref, bf_ref, cf_ref, dtf_ref, xb_ref, bb_ref, cb_ref, dtb_ref, alog_ref,
                yf_ref, yb_ref, sf_ref, sb_ref):
    @pl.when(pl.program_id(1) == 0)
    def _():
        sf_ref[...] = jnp.zeros_like(sf_ref)
        sb_ref[...] = jnp.zeros_like(sb_ref)

    _ssd_step(xf_ref, bf_ref, cf_ref, dtf_ref, alog_ref, yf_ref, sf_ref, reverse=False)
    _ssd_step(xb_ref, bb_ref, cb_ref, dtb_ref, alog_ref, yb_ref, sb_ref, reverse=True)


def _ssd(xs, bm, cm, dt, alog_row, chunk):
    bsz, seq, d_inner = xs.shape
    gn = bm.shape[2]
    nc = seq // chunk
    n_state = gn // SSM_GROUPS
    fw = lambda w: pl.BlockSpec((1, chunk, w), lambda b, c: (b, c, 0))
    bw = lambda w: pl.BlockSpec((1, chunk, w), lambda b, c: (b, nc - 1 - c, 0))
    state = pltpu.VMEM((SSM_GROUPS, n_state, d_inner // SSM_GROUPS), F32)
    return pl.pallas_call(
        _ssd_kernel,
        grid=(bsz, nc),
        in_specs=[fw(d_inner), fw(gn), fw(gn), fw(LANES), bw(d_inner), bw(gn), bw(gn), bw(LANES),
                  _const_spec((1, LANES))],
        out_specs=[fw(d_inner), bw(d_inner)],
        out_shape=[jax.ShapeDtypeStruct(xs.shape, BF16)] * 2,
        scratch_shapes=[state, state],
        compiler_params=_cparams("parallel", "arbitrary"),
        name="ssm_ssd",
    )(xs, bm, cm, dt, xs, bm, cm, dt, alog_row)


def _ssm_out_kernel(yf_ref, yb_ref, xs_ref, z_ref, h_ref, dsk_ref, nw_ref, w_ref, out_ref, slab_ref, *, group_w):
    d_inner = z_ref.shape[1]
    parts = []
    for c0 in range(0, d_inner, group_w):
        cs = slice(c0, c0 + group_w)
        y = (yf_ref[:, cs].astype(F32) + yb_ref[:, cs].astype(F32)
             + xs_ref[:, cs].astype(F32) * dsk_ref[:, cs]) * z_ref[:, cs].astype(F32)
        parts.append(_rmsnorm(y, nw_ref[:, cs]).astype(BF16))
    y = jnp.concatenate(parts, axis=1)
    _deinterleave_add(h_ref, slab_ref, out_ref, _dot(y, w_ref[...]))


def _ssm_out(y_fw, y_bw, xs, z, h2, d_skip_row, norm_w, w_out_bf, tm):
    t, d = h2.shape
    d_inner = z.shape[1]
    wide = pl.BlockSpec((tm, d_inner), lambda i: (i, 0))
    tile = pl.BlockSpec((tm, d), lambda i: (i, 0))
    return pl.pallas_call(
        functools.partial(_ssm_out_kernel, group_w=d_inner // SSM_GROUPS),
        grid=(t // tm,),
        in_specs=[wide, wide, wide, wide, tile, _const_spec((1, d_inner)), _const_spec((1, d_inner)),
                  _const_spec(w_out_bf.shape)],
        out_specs=tile,
        out_shape=jax.ShapeDtypeStruct((t, d), F32),
        scratch_shapes=[pltpu.VMEM((d // LANES, tm, LANES), F32)],
        compiler_params=_cparams("parallel"),
        name="ssm_out",
    )(y_fw, y_bw, xs, z, h2, d_skip_row, norm_w.reshape(1, d_inner), w_out_bf)


def kernel(x, norm1_w, norm2_w, a_w_in, a_lb_logits, a_norm_w, a_w_out, b_w_in, b_conv_w, b_conv_b,
           b_dt_bias, b_a_log, b_d_skip, b_norm_w, b_w_out, ffn_w_in, ffn_conv_w, ffn_conv_b, ffn_w_out,
           final_norm_w):
    bsz, seq, d = x.shape
    depth = norm1_w.shape[0]
    t = bsz * seq
    tm = min(512, seq)
    d_inner = b_norm_w.shape[1]
    n_ssm_heads = b_dt_bias.shape[2]
    assert seq % tm == 0 and seq % GLA_CHUNK == 0 and tm % SSD_CHUNK == 0
    assert 2 * n_ssm_heads <= LANES and d_inner == n_ssm_heads * SSM_HEAD_DIM

    h = x
    for i in range(depth):
        j = i // 2
        if i % 2 == 0:
            q, lh_fw, lm_fw, k_fw, lh_bw, lm_bw, k_bw, v, g = _hgrn_proj(
                h.reshape(t, d), norm1_w[i], a_w_in[j].astype(BF16), a_lb_logits, i, tm)
            r3 = lambda a: a.reshape(bsz, seq, d)
            o_fw, o_bw = _gla(r3(q), r3(lh_fw), r3(lm_fw), r3(k_fw), r3(lh_bw), r3(lm_bw), r3(k_bw), r3(v),
                              GLA_CHUNK)
            h = _hgrn_out(o_fw.reshape(t, d), o_bw.reshape(t, d), g, h.reshape(t, d), a_norm_w[j],
                          a_w_out[j].astype(BF16), tm).reshape(bsz, seq, d)
        else:
            w_in = b_w_in[j].astype(BF16)
            conv_dim = b_conv_w.shape[2]
            pad = LANES - 2 * n_ssm_heads
            wdt = jnp.pad(w_in[:, d_inner + conv_dim:], ((0, 0), (0, pad)))
            dt_bias_row = jnp.pad(b_dt_bias[j].reshape(1, -1), ((0, 0), (0, pad)))
            alog_row = jnp.pad(b_a_log[j].reshape(1, -1), ((0, 0), (0, pad)))
            z, xs, bm, cm, dt = _ssm_proj(h, norm1_w[i], w_in, wdt, b_conv_w[j], b_conv_b[j], dt_bias_row,
                                          d_inner, tm)
            y_fw, y_bw = _ssd(xs, bm, cm, dt, alog_row, SSD_CHUNK)
            d_skip_row = jnp.repeat(b_d_skip[j], SSM_HEAD_DIM).reshape(1, d_inner)
            f2 = lambda a: a.reshape(t, d_inner)
            h = _ssm_out(f2(y_fw), f2(y_bw), f2(xs), f2(z), h.reshape(t, d), d_skip_row, b_norm_w[j],
                         b_w_out[j].astype(BF16), tm).reshape(bsz, seq, d)
        h = _ffn(h, norm2_w[i], ffn_w_in[i].astype(BF16), ffn_conv_w[i], ffn_conv_b[i],
                 ffn_w_out[i].astype(BF16), final_norm_w, min(2 * tm, seq), final_norm=(i == depth - 1))
    return h
```

```python
import functools

import jax
import jax.numpy as jnp
from jax import lax
from jax.experimental import pallas as pl
from jax.experimental.pallas import tpu as pltpu

F32 = jnp.float32
BF16 = jnp.bfloat16

EPS = 1e-6
HG_HEAD_DIM = 128
SSM_HEAD_DIM = 64
SSM_GROUPS = 8
SSM_CONV_HALO = 2
FFN_CONV_HALO = 1
SUBLANES = 8
LANES = 128
CHUNK = 128
CHUNK_STREAM = CHUNK // SUBLANES
LOG2E = 1.4426950408889634
GLA_SAFE_LOG_DECAY = 75.0
VMEM_LIMIT_BYTES = 56 * 1024 * 1024


def _cparams(*sem):
    return pltpu.CompilerParams(dimension_semantics=sem, vmem_limit_bytes=VMEM_LIMIT_BYTES)


def _const_spec(shape, index=None):
    index = (0,) * len(shape) if index is None else index
    return pl.BlockSpec(shape, lambda *_: index, pipeline_mode=pl.Buffered(1))


def _layer_spec(w, layer, n_col_blocks=1, col_block=0):
    _, k, n = w.shape
    return pl.BlockSpec((None, k, n // n_col_blocks), lambda *_: (layer, 0, col_block),
                        pipeline_mode=pl.Buffered(1))


def _wide_layer_spec(w, layer, width, col_block):
    return pl.BlockSpec((None, w.shape[1], width), lambda *_: (layer, 0, col_block), pipeline_mode=pl.Buffered(1))


def _rmsnorm(x, w):
    return x * lax.rsqrt(jnp.mean(x * x, axis=-1, keepdims=True) + EPS) * w


def _sigmoid(x):
    return 0.5 * jnp.tanh(0.5 * x) + 0.5


def _silu(x):
    h = 0.5 * x
    return h * jnp.tanh(h) + h


def _softplus(x):
    return jnp.maximum(x, 0.0) + jnp.log(1.0 + jnp.exp(-jnp.abs(x)))


def _dot(a, b):
    return jnp.dot(a, b, preferred_element_type=F32)


def _dot_nt(a, b):
    return lax.dot_general(a, b, (((1,), (1,)), ((), ())), preferred_element_type=F32)


def _dot_tn(a, b):
    return lax.dot_general(a, b, (((0,), (0,)), ((), ())), preferred_element_type=F32)


def _split3(x):
    hi = x.astype(BF16)
    r1 = x - hi.astype(F32)
    mid = r1.astype(BF16)
    lo = (r1 - mid.astype(F32)).astype(BF16)
    return hi, mid, lo


def _sel_rows(m01, x):
    hi, mid, lo = _split3(x)
    return _dot(m01, hi) + _dot(m01, mid) + _dot(m01, lo)


def _sel_cols(x, m01):
    hi, mid, lo = _split3(x)
    return _dot(hi, m01) + _dot(mid, m01) + _dot(lo, m01)


def _iota2(shape, dim):
    return lax.broadcasted_iota(jnp.int32, shape, dim)


def _chunk_token(idx):
    return (idx & (SUBLANES - 1)) * CHUNK_STREAM + (idx >> 3)


def _chunk_row(tok):
    return (tok % CHUNK_STREAM) * SUBLANES + tok // CHUNK_STREAM


def _chunk_running_sum(x, reverse):
    n = CHUNK // SUBLANES
    order = range(n - 1, -1, -1) if reverse else range(n)
    tiles, acc = [None] * n, None
    for i in order:
        t = x[i * SUBLANES:(i + 1) * SUBLANES]
        acc = t if acc is None else acc + t
        tiles[i] = acc
    sub = _iota2(acc.shape, 0)
    scan = acc
    for k in (1, 2, 4):
        if reverse:
            scan = scan + jnp.where(sub < SUBLANES - k, pltpu.roll(scan, SUBLANES - k, 0), 0.0)
        else:
            scan = scan + jnp.where(sub >= k, pltpu.roll(scan, k, 0), 0.0)
    offset = scan - acc
    return jnp.concatenate([t + offset for t in tiles], axis=0)


def _interleave_rows(x, slab_ref):
    tm, d = x.shape
    for s in range(d // LANES):
        for g in range(0, tm, CHUNK):
            for r in range(SUBLANES):
                slab_ref[s, pl.ds(g + r, CHUNK_STREAM, stride=SUBLANES), :] = (
                    x[g + r * CHUNK_STREAM: g + (r + 1) * CHUNK_STREAM, s * LANES:(s + 1) * LANES])
    return jnp.concatenate([slab_ref[s] for s in range(d // LANES)], axis=1)


def _deinterleave_add(res_ref, slab_ref, out_ref, y):
    tm, d = y.shape
    for s in range(d // LANES):
        slab_ref[s] = y[:, s * LANES:(s + 1) * LANES]
    for s in range(d // LANES):
        for g in range(0, tm, CHUNK):
            for r in range(SUBLANES):
                rows = slice(g + r * CHUNK_STREAM, g + (r + 1) * CHUNK_STREAM)
                lanes = slice(s * LANES, (s + 1) * LANES)
                out_ref[rows, lanes] = res_ref[rows, lanes] + slab_ref[s, pl.ds(g + r, CHUNK_STREAM, stride=SUBLANES), :]


def _hgrn_proj_kernel(h_ref, nw_ref, w_ref, lbl_ref, q_ref, lf_ref, kf_ref, lb_ref, kb_ref, v_ref, g_ref,
                      slab_ref, *, layer, col_chunk):
    d = h_ref.shape[1]
    u = _rmsnorm(_interleave_rows(h_ref[...], slab_ref), nw_ref[...]).astype(BF16)
    lg = lbl_ref[...]
    n_rows = lg.shape[0]
    mx = lg[0:1]
    for r in range(1, n_rows):
        mx = jnp.maximum(mx, lg[r:r + 1])
    es = [jnp.exp(lg[r:r + 1] - mx) for r in range(n_rows)]
    tot = es[0]
    for r in range(1, n_rows):
        tot = tot + es[r]
    part = es[0]
    for r in range(1, layer + 1):
        part = part + es[r]
    lb = part / tot

    def gate(fr, cols, l_ref, k_ref):
        lbc = lb[:, cols]
        f = lbc + (1.0 - lbc) * _sigmoid(fr)
        l_ref[:, cols] = jnp.log(f) * LOG2E
        k_ref[:, cols] = (1.0 - f).astype(BF16)

    for sec in range(5):
        for c0 in range(0, d, col_chunk):
            cols = slice(c0, c0 + col_chunk)
            p = _dot(u, w_ref[:, sec * d + c0: sec * d + c0 + col_chunk])
            if sec == 0:
                q_ref[:, cols] = _silu(p).astype(BF16)
            elif sec == 1:
                gate(p, cols, lf_ref, kf_ref)
            elif sec == 2:
                gate(p, cols, lb_ref, kb_ref)
            elif sec == 3:
                v_ref[:, cols] = p.astype(BF16)
            else:
                g_ref[:, cols] = _silu(p).astype(BF16)


def _hgrn_proj(h2, norm_w, w_in_bf, j, lb_logits, layer, tm):
    t, d = h2.shape
    tile = pl.BlockSpec((tm, d), lambda i: (i, 0))
    tok = lambda dt: jax.ShapeDtypeStruct((t, d), dt)
    return pl.pallas_call(
        functools.partial(_hgrn_proj_kernel, layer=layer, col_chunk=min(512, d)),
        grid=(t // tm,),
        in_specs=[tile, _const_spec((1, d)), _layer_spec(w_in_bf, j), _const_spec(lb_logits.shape)],
        out_specs=[tile] * 7,
        out_shape=[tok(BF16), tok(F32), tok(BF16), tok(F32), tok(BF16), tok(BF16), tok(BF16)],
        scratch_shapes=[pltpu.VMEM((d // LANES, tm, LANES), F32)],
        compiler_params=_cparams("parallel"),
        name="hgrn_proj",
    )(h2, norm_w.reshape(1, d), w_in_bf, lb_logits)


def _gla_step(q_ref, k_ref, v_ref, o_ref, st_ref, *, reverse, safe, b_all):
    c = q_ref.shape[1]
    n_heads = st_ref.shape[0]
    hd = st_ref.shape[2]
    row = _chunk_token(_iota2((c, c), 0))
    col = _chunk_token(_iota2((c, c), 1))
    last = 0 if reverse else c - 1
    mid = _chunk_row(c // 2 if reverse else c // 2 - 1)
    for h in range(n_heads):
        hs = slice(h * hd, (h + 1) * hd)
        b = b_all[:, hs]
        bl = b[last:last + 1, :]
        vb = v_ref[0, :, hs]
        st = st_ref[h]
        if safe:
            bm = b[mid:mid + 1, :]
            dm = b - bm
            q_hat = q_ref[0, :, hs] * jnp.exp2(dm).astype(BF16)
            k_hat = k_ref[0, :, hs] * jnp.exp2(-dm).astype(BF16)
            keep = (row <= col) if reverse else (row >= col)
            att = jnp.where(keep, _dot_nt(q_hat, k_hat), 0.0)
            o = _dot_nt(q_hat, (st * jnp.exp2(bm)).astype(BF16)) + _dot(att.astype(BF16), vb)
            kv = _dot_tn(vb, k_hat) * jnp.exp2(bl - bm)
        else:
            qf = q_ref[0, :, hs].astype(F32)
            kf = k_ref[0, :, hs].astype(F32)
            att = jnp.where(row == col, _dot_nt(q_ref[0, :, hs], k_ref[0, :, hs]), 0.0)
            lvl = 0
            while (1 << lvl) < c:
                bs = 1 << lvl
                edge = (row >> (lvl + 1)) * (2 * bs) + (bs if reverse else bs - 1)
                g = _sel_rows((col == edge).astype(BF16), b)
                q_l = (qf * jnp.exp2(jnp.minimum(b - g, 0.0))).astype(BF16)
                k_l = (kf * jnp.exp2(jnp.minimum(g - b, 0.0))).astype(BF16)
                same_parent = (row >> (lvl + 1)) == (col >> (lvl + 1))
                if reverse:
                    pair = same_parent & ((col >> lvl) == (row >> lvl) + 1)
                else:
                    pair = same_parent & ((row >> lvl) == (col >> lvl) + 1)
                att = att + jnp.where(pair, _dot_nt(q_l, k_l), 0.0)
                lvl += 1
            q_in = (qf * jnp.exp2(b)).astype(BF16)
            o = _dot_nt(q_in, st.astype(BF16)) + _dot(att.astype(BF16), vb)
            kv = _dot_tn(vb, (kf * jnp.exp2(bl - b)).astype(BF16))
        o_ref[0, :, hs] = o.astype(o_ref.dtype)
        st_ref[h] = st * jnp.exp2(bl) + kv


def _gla_kernel(qf_ref, lf_ref, kf_ref, vf_ref, qb_ref, lb_ref, kb_ref, vb_ref, of_ref, ob_ref, sf_ref, sb_ref):
    @pl.when(pl.program_id(1) == 0)
    def _():
        sf_ref[...] = jnp.zeros_like(sf_ref)
        sb_ref[...] = jnp.zeros_like(sb_ref)

    def half_chunk_decay(l_ref):
        tot = l_ref[0, 0:SUBLANES, :]
        for i in range(1, CHUNK // SUBLANES):
            tot = tot + l_ref[0, i * SUBLANES:(i + 1) * SUBLANES, :]
        s = tot + pltpu.roll(tot, 1, 0)
        s = s + pltpu.roll(s, 2, 0)
        lo, hi = SUBLANES // 2 - 1, SUBLANES - 1
        return jnp.minimum(s[lo:lo + 1, :], s[hi:hi + 1, :])

    worst = jnp.min(jnp.minimum(half_chunk_decay(lf_ref), half_chunk_decay(lb_ref)))
    safe = worst >= -GLA_SAFE_LOG_DECAY * LOG2E
    b_fw = _chunk_running_sum(lf_ref[0], reverse=False)
    b_bw = _chunk_running_sum(lb_ref[0], reverse=True)

    for flag, pred in ((True, safe), (False, jnp.logical_not(safe))):
        @pl.when(pred)
        def _(flag=flag):
            _gla_step(qf_ref, kf_ref, vf_ref, of_ref, sf_ref, reverse=False, safe=flag, b_all=b_fw)
            _gla_step(qb_ref, kb_ref, vb_ref, ob_ref, sb_ref, reverse=True, safe=flag, b_all=b_bw)


def _gla(q, l_fw, k_fw, l_bw, k_bw, v, chunk):
    bsz, seq, d = q.shape
    nc = seq // chunk
    n_heads = d // HG_HEAD_DIM
    fw = pl.BlockSpec((1, chunk, d), lambda b, c: (b, c, 0))
    bw = pl.BlockSpec((1, chunk, d), lambda b, c: (b, nc - 1 - c, 0))
    state = pltpu.VMEM((n_heads, HG_HEAD_DIM, HG_HEAD_DIM), F32)
    return pl.pallas_call(
        _gla_kernel,
        grid=(bsz, nc),
        in_specs=[fw] * 4 + [bw] * 4,
        out_specs=[fw, bw],
        out_shape=[jax.ShapeDtypeStruct((bsz, seq, d), BF16)] * 2,
        scratch_shapes=[state, state],
        compiler_params=_cparams("parallel", "arbitrary"),
        name="hgrn_gla",
    )(q, l_fw, k_fw, v, q, l_bw, k_bw, v)


def _hgrn_out_kernel(of_ref, ob_ref, g_ref, h_ref, nw_ref, w_ref, out_ref, slab_ref):
    d = h_ref.shape[1]
    nw = nw_ref[...]
    parts = []
    for c0 in range(0, d, HG_HEAD_DIM):
        hs = slice(c0, c0 + HG_HEAD_DIM)
        o = of_ref[:, hs].astype(F32) + ob_ref[:, hs].astype(F32)
        y = _rmsnorm(o, nw) * g_ref[:, hs].astype(F32)
        parts.append(y.astype(BF16))
    y = jnp.concatenate(parts, axis=1)
    _deinterleave_add(h_ref, slab_ref, out_ref, _dot(y, w_ref[...]))


def _hgrn_out(o_fw, o_bw, g, h2, norm_w, w_out_bf, j, tm):
    t, d = h2.shape
    tile = pl.BlockSpec((tm, d), lambda i: (i, 0))
    return pl.pallas_call(
        _hgrn_out_kernel,
        grid=(t // tm,),
        in_specs=[tile, tile, tile, tile, _const_spec((1, HG_HEAD_DIM)), _layer_spec(w_out_bf, j)],
        out_specs=tile,
        out_shape=jax.ShapeDtypeStruct((t, d), F32),
        scratch_shapes=[pltpu.VMEM((d // LANES, tm, LANES), F32)],
        compiler_params=_cparams("parallel"),
        name="hgrn_out",
    )(o_fw, o_bw, g, h2, norm_w.reshape(1, HG_HEAD_DIM), w_out_bf)


def _halo_rows(hp_ref, hn_ref, nw):
    up = _rmsnorm(hp_ref[0], nw)
    un = _rmsnorm(hn_ref[0], nw)
    return jnp.concatenate([up, un], axis=0).astype(BF16)


def _halo_inside():
    i = pl.program_id(1)
    r = _iota2((2 * SUBLANES, 1), 0)
    has_prev = (i > 0).astype(F32)
    has_next = (i < pl.num_programs(1) - 1).astype(F32)
    return jnp.where(r < SUBLANES, has_prev, has_next) > 0.5


def _shifted(e_ref, x, xh, halo, n_rows):
    e_ref[0:SUBLANES, :] = xh[0:SUBLANES]
    e_ref[SUBLANES:SUBLANES + n_rows, :] = x
    e_ref[SUBLANES + n_rows:, :] = xh[SUBLANES:]
    return [x if j == halo else e_ref[pl.ds(SUBLANES - halo + j, n_rows), :] for j in range(2 * halo + 1)]


def _ffn_kernel(h_ref, hp_ref, hn_ref, nw_ref, wg_ref, wv_ref, cw_ref, cb_ref, wo_ref, fw_ref,
                out_ref, a_ref, e_ref, *, col_chunk, final_norm):
    tm = h_ref.shape[1]
    d_ff = wg_ref.shape[1]
    nw = nw_ref[...]
    hx = h_ref[0]
    u = _rmsnorm(hx, nw).astype(BF16)
    uh = _halo_rows(hp_ref, hn_ref, nw)
    inside = _halo_inside()
    for c0 in range(0, d_ff, col_chunk):
        cols = slice(c0, c0 + col_chunk)
        gate = _dot(u, wg_ref[:, cols])
        gh = jnp.where(inside, _dot(uh, wg_ref[:, cols]), 0.0)
        taps = _shifted(e_ref, gate, gh, FFN_CONV_HALO, tm)
        conv = cb_ref[:, cols]
        for j, tap in enumerate(taps):
            conv = conv + cw_ref[j:j + 1, cols] * tap
        val = _dot(u, wv_ref[:, cols])
        a_ref[:, cols] = (_silu(conv) * val).astype(BF16)
    y = hx + _dot(a_ref[...], wo_ref[...])
    if final_norm:
        y = _rmsnorm(y, fw_ref[...])
    out_ref[0] = y


def _halo_specs(tm, d, seq):
    nb = tm // SUBLANES
    n_blocks = seq // SUBLANES
    prev = pl.BlockSpec((1, SUBLANES, d), lambda b, i: (b, jnp.maximum(i * nb - 1, 0), 0))
    nxt = pl.BlockSpec((1, SUBLANES, d), lambda b, i: (b, jnp.minimum((i + 1) * nb, n_blocks - 1), 0))
    return prev, nxt


def _ffn(h3, norm_w, w_in_bf, conv_w, conv_b, wo_bf, layer, final_w, tm, final_norm):
    bsz, seq, d = h3.shape
    d_ff = wo_bf.shape[1]
    col_chunk = 256
    tile = pl.BlockSpec((1, tm, d), lambda b, i: (b, i, 0))
    prev, nxt = _halo_specs(tm, d, seq)
    return pl.pallas_call(
        functools.partial(_ffn_kernel, col_chunk=col_chunk, final_norm=final_norm),
        grid=(bsz, seq // tm),
        in_specs=[tile, prev, nxt, _const_spec((1, d)), _layer_spec(w_in_bf, layer, 2, 0),
                  _layer_spec(w_in_bf, layer, 2, 1), _const_spec(conv_w.shape), _const_spec((1, d_ff)),
                  _layer_spec(wo_bf, layer), _const_spec((1, d))],
        out_specs=tile,
        out_shape=jax.ShapeDtypeStruct(h3.shape, F32),
        scratch_shapes=[pltpu.VMEM((tm, d_ff), BF16), pltpu.VMEM((tm + 2 * SUBLANES, col_chunk), F32)],
        compiler_params=_cparams("parallel", "parallel"),
        name="convglu_final" if final_norm else "convglu",
    )(h3, h3, h3, norm_w.reshape(1, d), w_in_bf, w_in_bf, conv_w, conv_b.reshape(1, d_ff), wo_bf,
      final_w.reshape(1, d))


def _interleaved_taps(x, xh, halo):
    tm, w = x.shape
    nv = CHUNK // SUBLANES
    n_chunks = tm // CHUNK
    sub = _iota2((SUBLANES, w), 0)
    tile = lambda g, j: x[g * CHUNK + j * SUBLANES: g * CHUNK + (j + 1) * SUBLANES]

    def after(g, j):
        if g + 1 < n_chunks:
            return tile(g + 1, j)
        nxt = xh[SUBLANES:]
        return nxt if j == 0 else pltpu.roll(nxt, SUBLANES - j, 0)

    def before(g, j):
        if g > 0:
            return tile(g - 1, j)
        prv = xh[:SUBLANES]
        return prv if j == nv - 1 else pltpu.roll(prv, nv - 1 - j, 0)

    taps = [[] for _ in range(2 * halo + 1)]
    for g in range(n_chunks):
        lead = [pltpu.roll(jnp.where(sub == SUBLANES - 1, before(g, nv - halo + m), tile(g, nv - halo + m)), 1, 0)
                for m in range(halo)]
        tail = [pltpu.roll(jnp.where(sub == 0, after(g, m), tile(g, m)), SUBLANES - 1, 0) for m in range(halo)]
        ext = jnp.concatenate(lead + [x[g * CHUNK:(g + 1) * CHUNK]] + tail, axis=0)
        for j in range(2 * halo + 1):
            taps[j].append(ext[j * SUBLANES: j * SUBLANES + CHUNK])
    return [jnp.concatenate(t, axis=0) for t in taps]


def _ssm_proj_kernel(h_ref, hp_ref, hn_ref, nw_ref, wz_ref, wx_ref, wbc_ref, wdt_ref, cw_ref, cb_ref, dtb_ref,
                     z_ref, xs_ref, bm_ref, cm_ref, dt_ref, slab_ref, *, col_chunk):
    d_inner = z_ref.shape[2]
    gn = bm_ref.shape[2]
    nw = nw_ref[...]
    u = _rmsnorm(_interleave_rows(h_ref[0], slab_ref), nw).astype(BF16)
    uh = _halo_rows(hp_ref, hn_ref, nw)
    inside = _halo_inside()
    for c0 in range(0, d_inner, col_chunk):
        cols = slice(c0, c0 + col_chunk)
        z_ref[0, :, cols] = _silu(_dot(u, wz_ref[:, cols])).astype(BF16)
    for c0 in range(0, d_inner + 2 * gn, col_chunk):
        cols = slice(c0, c0 + col_chunk)
        w = wx_ref[:, cols] if c0 < d_inner else wbc_ref[:, c0 - d_inner: c0 - d_inner + col_chunk]
        x = _dot(u, w)
        xh = jnp.where(inside, _dot(uh, w), 0.0)
        taps = _interleaved_taps(x, xh, SSM_CONV_HALO)
        conv = cb_ref[:, cols]
        for j, tap in enumerate(taps):
            conv = conv + cw_ref[j:j + 1, cols] * tap
        act = _silu(conv).astype(BF16)
        if c0 < d_inner:
            xs_ref[0, :, cols] = act
        elif c0 < d_inner + gn:
            bm_ref[0, :, c0 - d_inner: c0 - d_inner + col_chunk] = act
        else:
            cm_ref[0, :, c0 - d_inner - gn: c0 - d_inner - gn + col_chunk] = act
    dt_ref[0] = _softplus(_dot(u, wdt_ref[...]) + dtb_ref[...])


def _ssm_proj(h3, norm_w, w_in_bf, j, wdt_bf, conv_w, conv_b, dt_bias_row, d_inner, tm):
    bsz, seq, d = h3.shape
    conv_dim = conv_w.shape[1]
    gn = (conv_dim - d_inner) // 2
    assert 2 * gn == d_inner
    col_chunk = 512
    tile = pl.BlockSpec((1, tm, d), lambda b, i: (b, i, 0))
    prev, nxt = _halo_specs(tm, d, seq)
    out_tile = lambda w: pl.BlockSpec((1, tm, w), lambda b, i: (b, i, 0))
    tok = lambda w, dt: jax.ShapeDtypeStruct((bsz, seq, w), dt)
    return pl.pallas_call(
        functools.partial(_ssm_proj_kernel, col_chunk=col_chunk),
        grid=(bsz, seq // tm),
        in_specs=[tile, prev, nxt, _const_spec((1, d)), _wide_layer_spec(w_in_bf, j, d_inner, 0),
                  _wide_layer_spec(w_in_bf, j, d_inner, 1), _wide_layer_spec(w_in_bf, j, d_inner, 2),
                  _const_spec(wdt_bf.shape), _const_spec(conv_w.shape), _const_spec((1, conv_dim)),
                  _const_spec((1, LANES))],
        out_specs=[out_tile(d_inner), out_tile(d_inner), out_tile(gn), out_tile(gn), out_tile(LANES)],
        out_shape=[tok(d_inner, BF16), tok(d_inner, BF16), tok(gn, BF16), tok(gn, BF16), tok(LANES, F32)],
        scratch_shapes=[pltpu.VMEM((d // LANES, tm, LANES), F32)],
        compiler_params=_cparams("parallel", "parallel"),
        name="ssm_proj",
    )(h3, h3, h3, norm_w.reshape(1, d), w_in_bf, w_in_bf, w_in_bf, wdt_bf, conv_w, conv_b.reshape(1, conv_dim),
      dt_bias_row)


def _ssd_step(xs_ref, bm_ref, cm_ref, dt_ref, alog_ref, y_ref, st_ref, *, reverse, skip_ref=None):
    c = xs_ref.shape[1]
    n_groups = st_ref.shape[0]
    n_state = st_ref.shape[1]
    gw = st_ref.shape[2]
    heads = n_groups * (gw // SSM_HEAD_DIM)
    h0 = heads if reverse else 0
    row = _chunk_token(_iota2((c, c), 0))
    col = _chunk_token(_iota2((c, c), 1))
    lane = _iota2((c, LANES), 1)
    keep = (row <= col) if reverse else (row >= col)
    last = 0 if reverse else c - 1

    dt = dt_ref[0]
    la = dt * (-LOG2E * jnp.exp(alog_ref[...]))
    tri = keep.astype(BF16)
    a_col = _sel_rows(tri, la)
    a_row = _sel_cols(la.T, tri.T)
    a_src = a_row - jnp.log(dt.T) * LOG2E
    w_row_bf = jnp.exp2(a_row[:, last:last + 1] - a_src).astype(BF16)

    hpg = gw // SSM_HEAD_DIM
    lane_g = _iota2((c, gw), 1)
    for g in range(n_groups):
        bg = bm_ref[0, :, g * n_state:(g + 1) * n_state]
        cg = cm_ref[0, :, g * n_state:(g + 1) * n_state]
        bt = bg.astype(F32).T.astype(BF16)
        cb = _dot(cg, bt).astype(BF16)
        xg = xs_ref[0, :, g * gw:(g + 1) * gw]
        zero = jnp.zeros_like(xg)
        ws, bws, bcs, x_blocks = [], [], [], []
        for j in range(hpg):
            hh = h0 + g * hpg + j
            bc = jnp.broadcast_to(a_col[:, hh:hh + 1], (c, c))
            decay = jnp.exp2(jnp.where(keep, bc - a_src[hh:hh + 1, :], -jnp.inf))
            ws.append(cb * decay.astype(BF16))
            bws.append(bt * w_row_bf[hh:hh + 1, :])
            bcs.append(bc)
            x_blocks.append(jnp.where((lane_g >= j * SSM_HEAD_DIM) & (lane_g < (j + 1) * SSM_HEAD_DIM), xg, zero))
        lhs = jnp.concatenate([jnp.concatenate(ws, axis=1), jnp.concatenate(bws, axis=1)], axis=0)
        res = _dot(lhs, jnp.concatenate(x_blocks, axis=0))
        st = st_ref[g]
        y_in = _dot(cg, st.astype(BF16))
        e_parts = []
        for p in range(gw // LANES):
            e_parts.append(jnp.exp2(jnp.where(lane < SSM_HEAD_DIM, bcs[2 * p], bcs[2 * p + 1])))
        e_t = jnp.concatenate(e_parts, axis=1)
        y = res[:c] + y_in * e_t
        if skip_ref is not None:
            y = y + xg.astype(F32) * skip_ref[:, g * gw:(g + 1) * gw]
        y_ref[0, :, g * gw:(g + 1) * gw] = y.astype(y_ref.dtype)
        st_ref[g] = st * e_t[last:last + 1, :] + res[c:]


def _ssd_kernel(xf_ref, bf_ref, cf_ref, dtf_ref, xb_ref, bb_ref, cb_ref, dtb_ref, alog_ref, dsk_ref,
                yf_ref, yb_ref, sf_ref, sb_ref):
    @pl.when(pl.program_id(1) == 0)
    def _():
        sf_ref[...] = jnp.zeros_like(sf_ref)
        sb_ref[...] = jnp.zeros_like(sb_ref)

    _ssd_step(xf_ref, bf_ref, cf_ref, dtf_ref, alog_ref, yf_ref, sf_ref, reverse=False, skip_ref=dsk_ref)
    _ssd_step(xb_ref, bb_ref, cb_ref, dtb_ref, alog_ref, yb_ref, sb_ref, reverse=True)


def _ssd(xs, bm, cm, dt, alog_row, d_skip_row, chunk):
    bsz, seq, d_inner = xs.shape
    gn = bm.shape[2]
    nc = seq // chunk
    n_state = gn // SSM_GROUPS
    fw = lambda w: pl.BlockSpec((1, chunk, w), lambda b, c: (b, c, 0))
    bw = lambda w: pl.BlockSpec((1, chunk, w), lambda b, c: (b, nc - 1 - c, 0))
    state = pltpu.VMEM((SSM_GROUPS, n_state, d_inner // SSM_GROUPS), F32)
    return pl.pallas_call(
        _ssd_kernel,
        grid=(bsz, nc),
        in_specs=[fw(d_inner), fw(gn), fw(gn), fw(LANES), bw(d_inner), bw(gn), bw(gn), bw(LANES),
                  _const_spec((1, LANES)), _const_spec((1, d_inner))],
        out_specs=[fw(d_inner), bw(d_inner)],
        out_shape=[jax.ShapeDtypeStruct(xs.shape, BF16)] * 2,
        scratch_shapes=[state, state],
        compiler_params=_cparams("parallel", "arbitrary"),
        name="ssm_ssd",
    )(xs, bm, cm, dt, xs, bm, cm, dt, alog_row, d_skip_row)


def _ssm_out_kernel(yf_ref, yb_ref, z_ref, h_ref, nw_ref, w_ref, out_ref, slab_ref, *, group_w):
    d_inner = z_ref.shape[1]
    parts = []
    for c0 in range(0, d_inner, group_w):
        cs = slice(c0, c0 + group_w)
        y = (yf_ref[:, cs].astype(F32) + yb_ref[:, cs].astype(F32)) * z_ref[:, cs].astype(F32)
        parts.append(_rmsnorm(y, nw_ref[:, cs]).astype(BF16))
    y = jnp.concatenate(parts, axis=1)
    _deinterleave_add(h_ref, slab_ref, out_ref, _dot(y, w_ref[...]))


def _ssm_out(y_fw, y_bw, z, h2, norm_w, w_out_bf, j, tm):
    t, d = h2.shape
    d_inner = z.shape[1]
    wide = pl.BlockSpec((tm, d_inner), lambda i: (i, 0))
    tile = pl.BlockSpec((tm, d), lambda i: (i, 0))
    return pl.pallas_call(
        functools.partial(_ssm_out_kernel, group_w=d_inner // SSM_GROUPS),
        grid=(t // tm,),
        in_specs=[wide, wide, wide, tile, _const_spec((1, d_inner)), _layer_spec(w_out_bf, j)],
        out_specs=tile,
        out_shape=jax.ShapeDtypeStruct((t, d), F32),
        scratch_shapes=[pltpu.VMEM((d // LANES, tm, LANES), F32)],
        compiler_params=_cparams("parallel"),
        name="ssm_out",
    )(y_fw, y_bw, z, h2, norm_w.reshape(1, d_inner), w_out_bf)


def kernel(x, norm1_w, norm2_w, a_w_in, a_lb_logits, a_norm_w, a_w_out, b_w_in, b_conv_w, b_conv_b,
           b_dt_bias, b_a_log, b_d_skip, b_norm_w, b_w_out, ffn_w_in, ffn_conv_w, ffn_conv_b, ffn_w_out,
           final_norm_w):
    bsz, seq, d = x.shape
    depth = norm1_w.shape[0]
    t = bsz * seq
    tm = min(512, seq)
    d_inner = b_norm_w.shape[1]
    n_ssm_heads = b_dt_bias.shape[2]
    assert seq % tm == 0 and tm % CHUNK == 0
    assert 2 * n_ssm_heads <= LANES and d_inner == n_ssm_heads * SSM_HEAD_DIM

    bf = lambda w: w.astype(BF16)
    a_w_in, a_w_out, b_w_in, b_w_out, ffn_w_in, ffn_w_out = map(bf, (a_w_in, a_w_out, b_w_in, b_w_out, ffn_w_in, ffn_w_out))
    h = x
    for i in range(depth):
        j = i // 2
        if i % 2 == 0:
            q, l_fw, k_fw, l_bw, k_bw, v, g = _hgrn_proj(h.reshape(t, d), norm1_w[i], a_w_in, j, a_lb_logits, i, tm)
            r3 = lambda a: a.reshape(bsz, seq, d)
            o_fw, o_bw = _gla(r3(q), r3(l_fw), r3(k_fw), r3(l_bw), r3(k_bw), r3(v), CHUNK)
            h = _hgrn_out(o_fw.reshape(t, d), o_bw.reshape(t, d), g, h.reshape(t, d), a_norm_w[j], a_w_out, j,
                          tm).reshape(bsz, seq, d)
        else:
            conv_dim = b_conv_w.shape[2]
            pad = LANES - 2 * n_ssm_heads
            wdt = jnp.pad(b_w_in[j, :, d_inner + conv_dim:], ((0, 0), (0, pad)))
            dt_bias_row = jnp.pad(b_dt_bias[j].reshape(1, -1), ((0, 0), (0, pad)))
            alog_row = jnp.pad(b_a_log[j].reshape(1, -1), ((0, 0), (0, pad)))
            z, xs, bm, cm, dt = _ssm_proj(h, norm1_w[i], b_w_in, j, wdt, b_conv_w[j], b_conv_b[j], dt_bias_row,
                                          d_inner, tm)
            d_skip_row = jnp.repeat(b_d_skip[j], SSM_HEAD_DIM).reshape(1, d_inner)
            y_fw, y_bw = _ssd(xs, bm, cm, dt, alog_row, d_skip_row, CHUNK)
            f2 = lambda a: a.reshape(t, d_inner)
            h = _ssm_out(f2(y_fw), f2(y_bw), f2(z), h.reshape(t, d), b_norm_w[j], b_w_out, j,
                         tm).reshape(bsz, seq, d)
        h = _ffn(h, norm2_w[i], ffn_w_in, ffn_conv_w[i], ffn_conv_b[i], ffn_w_out, i, final_norm_w,
                 min(2 * tm, seq), final_norm=(i == depth - 1))
    return h
```

```python
import functools

import jax
import jax.numpy as jnp
from jax import lax
from jax.experimental import pallas as pl
from jax.experimental.pallas import tpu as pltpu

F32 = jnp.float32
BF16 = jnp.bfloat16

EPS = 1e-6
HG_HEAD_DIM = 128
SSM_HEAD_DIM = 64
SSM_GROUPS = 8
SSM_CONV_HALO = 2
FFN_CONV_HALO = 1
SUBLANES = 8
LANES = 128
CHUNK = 128
CHUNK_STREAM = CHUNK // SUBLANES
LOG2E = 1.4426950408889634
GLA_SAFE_LOG_DECAY = 75.0
VMEM_LIMIT_BYTES = 56 * 1024 * 1024


def _cparams(*sem):
    return pltpu.CompilerParams(dimension_semantics=sem, vmem_limit_bytes=VMEM_LIMIT_BYTES)


def _const_spec(shape, index=None):
    index = (0,) * len(shape) if index is None else index
    return pl.BlockSpec(shape, lambda *_: index, pipeline_mode=pl.Buffered(1))


def _layer_spec(w, layer, n_col_blocks=1, col_block=0):
    _, k, n = w.shape
    return pl.BlockSpec((None, k, n // n_col_blocks), lambda *_: (layer, 0, col_block),
                        pipeline_mode=pl.Buffered(1))


def _wide_layer_spec(w, layer, width, col_block):
    return pl.BlockSpec((None, w.shape[1], width), lambda *_: (layer, 0, col_block), pipeline_mode=pl.Buffered(1))


def _rmsnorm(x, w):
    return x * lax.rsqrt(jnp.mean(x * x, axis=-1, keepdims=True) + EPS) * w


def _sigmoid(x):
    return 0.5 * jnp.tanh(0.5 * x) + 0.5


def _silu(x):
    h = 0.5 * x
    return h * jnp.tanh(h) + h


def _softplus(x):
    return jnp.maximum(x, 0.0) + jnp.log(1.0 + jnp.exp(-jnp.abs(x)))


def _dot(a, b):
    return jnp.dot(a, b, preferred_element_type=F32)


def _dot_nt(a, b):
    return lax.dot_general(a, b, (((1,), (1,)), ((), ())), preferred_element_type=F32)


def _dot_tn(a, b):
    return lax.dot_general(a, b, (((0,), (0,)), ((), ())), preferred_element_type=F32)


def _split3(x):
    hi = x.astype(BF16)
    r1 = x - hi.astype(F32)
    mid = r1.astype(BF16)
    lo = (r1 - mid.astype(F32)).astype(BF16)
    return hi, mid, lo


def _sel_rows(m01, x):
    hi, mid, lo = _split3(x)
    return _dot(m01, hi) + _dot(m01, mid) + _dot(m01, lo)


def _iota2(shape, dim):
    return lax.broadcasted_iota(jnp.int32, shape, dim)


def _chunk_token(idx):
    return (idx & (SUBLANES - 1)) * CHUNK_STREAM + (idx >> 3)


def _chunk_row(tok):
    return (tok % CHUNK_STREAM) * SUBLANES + tok // CHUNK_STREAM


def _chunk_running_sum(x, reverse):
    n = CHUNK // SUBLANES
    order = range(n - 1, -1, -1) if reverse else range(n)
    tiles, acc = [None] * n, None
    for i in order:
        t = x[i * SUBLANES:(i + 1) * SUBLANES]
        acc = t if acc is None else acc + t
        tiles[i] = acc
    sub = _iota2(acc.shape, 0)
    scan = acc
    for k in (1, 2, 4):
        if reverse:
            scan = scan + jnp.where(sub < SUBLANES - k, pltpu.roll(scan, SUBLANES - k, 0), 0.0)
        else:
            scan = scan + jnp.where(sub >= k, pltpu.roll(scan, k, 0), 0.0)
    offset = scan - acc
    return jnp.concatenate([t + offset for t in tiles], axis=0)


def _interleave_rows(x, slab_ref):
    tm, d = x.shape
    for s in range(d // LANES):
        for g in range(0, tm, CHUNK):
            for r in range(SUBLANES):
                slab_ref[s, pl.ds(g + r, CHUNK_STREAM, stride=SUBLANES), :] = (
                    x[g + r * CHUNK_STREAM: g + (r + 1) * CHUNK_STREAM, s * LANES:(s + 1) * LANES])
    return jnp.concatenate([slab_ref[s] for s in range(d // LANES)], axis=1)


def _deinterleave_add(res_ref, slab_ref, out_ref, y):
    tm, d = y.shape
    for s in range(d // LANES):
        slab_ref[s] = y[:, s * LANES:(s + 1) * LANES]
    for s in range(d // LANES):
        for g in range(0, tm, CHUNK):
            for r in range(SUBLANES):
                rows = slice(g + r * CHUNK_STREAM, g + (r + 1) * CHUNK_STREAM)
                lanes = slice(s * LANES, (s + 1) * LANES)
                out_ref[rows, lanes] = res_ref[rows, lanes] + slab_ref[s, pl.ds(g + r, CHUNK_STREAM, stride=SUBLANES), :]


def _hgrn_proj_kernel(h_ref, nw_ref, w_ref, lbl_ref, q_ref, lf_ref, kf_ref, lb_ref, kb_ref, v_ref, g_ref,
                      slab_ref, *, layer, col_chunk):
    d = h_ref.shape[1]
    u = _rmsnorm(_interleave_rows(h_ref[...], slab_ref), nw_ref[...]).astype(BF16)
    lg = lbl_ref[...]
    n_rows = lg.shape[0]
    mx = lg[0:1]
    for r in range(1, n_rows):
        mx = jnp.maximum(mx, lg[r:r + 1])
    es = [jnp.exp(lg[r:r + 1] - mx) for r in range(n_rows)]
    tot = es[0]
    for r in range(1, n_rows):
        tot = tot + es[r]
    part = es[0]
    for r in range(1, layer + 1):
        part = part + es[r]
    lb = part / tot

    def gate(fr, cols, l_ref, k_ref):
        lbc = lb[:, cols]
        f = lbc + (1.0 - lbc) * _sigmoid(fr)
        l_ref[:, cols] = jnp.log(f) * LOG2E
        k_ref[:, cols] = (1.0 - f).astype(BF16)

    for sec in range(5):
        for c0 in range(0, d, col_chunk):
            cols = slice(c0, c0 + col_chunk)
            p = _dot(u, w_ref[:, sec * d + c0: sec * d + c0 + col_chunk])
            if sec == 0:
                q_ref[:, cols] = _silu(p).astype(BF16)
            elif sec == 1:
                gate(p, cols, lf_ref, kf_ref)
            elif sec == 2:
                gate(p, cols, lb_ref, kb_ref)
            elif sec == 3:
                v_ref[:, cols] = p.astype(BF16)
            else:
                g_ref[:, cols] = _silu(p).astype(BF16)


def _hgrn_proj(h2, norm_w, w_in_bf, j, lb_logits, layer, tm):
    t, d = h2.shape
    tile = pl.BlockSpec((tm, d), lambda i: (i, 0))
    tok = lambda dt: jax.ShapeDtypeStruct((t, d), dt)
    return pl.pallas_call(
        functools.partial(_hgrn_proj_kernel, layer=layer, col_chunk=min(512, d)),
        grid=(t // tm,),
        in_specs=[tile, _const_spec((1, d)), _layer_spec(w_in_bf, j), _const_spec(lb_logits.shape)],
        out_specs=[tile] * 7,
        out_shape=[tok(BF16), tok(F32), tok(BF16), tok(F32), tok(BF16), tok(BF16), tok(BF16)],
        scratch_shapes=[pltpu.VMEM((d // LANES, tm, LANES), F32)],
        compiler_params=_cparams("parallel"),
        name="hgrn_proj",
    )(h2, norm_w.reshape(1, d), w_in_bf, lb_logits)


def _gla_step(q_ref, k_ref, v_ref, o_ref, st_ref, *, reverse, safe, b_all):
    c = q_ref.shape[1]
    n_heads = st_ref.shape[0]
    hd = st_ref.shape[2]
    row = _chunk_token(_iota2((c, c), 0))
    col = _chunk_token(_iota2((c, c), 1))
    last = 0 if reverse else c - 1
    mid = _chunk_row(c // 2 if reverse else c // 2 - 1)
    for h in range(n_heads):
        hs = slice(h * hd, (h + 1) * hd)
        b = b_all[:, hs]
        bl = b[last:last + 1, :]
        vb = v_ref[0, :, hs]
        st = st_ref[h]
        if safe:
            bm = b[mid:mid + 1, :]
            dm = b - bm
            q_hat = q_ref[0, :, hs] * jnp.exp2(dm).astype(BF16)
            k_hat = k_ref[0, :, hs] * jnp.exp2(-dm).astype(BF16)
            keep = (row <= col) if reverse else (row >= col)
            qk = _dot_nt(q_hat, jnp.concatenate([k_hat, (st * jnp.exp2(bm)).astype(BF16)], axis=0))
            att = jnp.where(keep, qk[:, :c], 0.0)
            o = qk[:, c:] + _dot(att.astype(BF16), vb)
            kv = _dot_tn(vb, k_hat) * jnp.exp2(bl - bm)
        else:
            qf = q_ref[0, :, hs].astype(F32)
            kf = k_ref[0, :, hs].astype(F32)
            att = jnp.where(row == col, _dot_nt(q_ref[0, :, hs], k_ref[0, :, hs]), 0.0)
            lvl = 0
            while (1 << lvl) < c:
                bs = 1 << lvl
                edge = (row >> (lvl + 1)) * (2 * bs) + (bs if reverse else bs - 1)
                g = _sel_rows((col == edge).astype(BF16), b)
                q_l = (qf * jnp.exp2(jnp.minimum(b - g, 0.0))).astype(BF16)
                k_l = (kf * jnp.exp2(jnp.minimum(g - b, 0.0))).astype(BF16)
                same_parent = (row >> (lvl + 1)) == (col >> (lvl + 1))
                if reverse:
                    pair = same_parent & ((col >> lvl) == (row >> lvl) + 1)
                else:
                    pair = same_parent & ((row >> lvl) == (col >> lvl) + 1)
                att = att + jnp.where(pair, _dot_nt(q_l, k_l), 0.0)
                lvl += 1
            q_in = (qf * jnp.exp2(b)).astype(BF16)
            o = _dot_nt(q_in, st.astype(BF16)) + _dot(att.astype(BF16), vb)
            kv = _dot_tn(vb, (kf * jnp.exp2(bl - b)).astype(BF16))
        o_ref[0, :, hs] = o.astype(o_ref.dtype)
        st_ref[h] = st * jnp.exp2(bl) + kv


def _gla_kernel(qf_ref, lf_ref, kf_ref, vf_ref, qb_ref, lb_ref, kb_ref, vb_ref, of_ref, ob_ref, sf_ref, sb_ref):
    @pl.when(pl.program_id(1) == 0)
    def _():
        sf_ref[...] = jnp.zeros_like(sf_ref)
        sb_ref[...] = jnp.zeros_like(sb_ref)

    def half_chunk_decay(l_ref):
        tot = l_ref[0, 0:SUBLANES, :]
        for i in range(1, CHUNK // SUBLANES):
            tot = tot + l_ref[0, i * SUBLANES:(i + 1) * SUBLANES, :]
        s = tot + pltpu.roll(tot, 1, 0)
        s = s + pltpu.roll(s, 2, 0)
        lo, hi = SUBLANES // 2 - 1, SUBLANES - 1
        return jnp.minimum(s[lo:lo + 1, :], s[hi:hi + 1, :])

    worst = jnp.min(jnp.minimum(half_chunk_decay(lf_ref), half_chunk_decay(lb_ref)))
    safe = worst >= -GLA_SAFE_LOG_DECAY * LOG2E
    b_fw = _chunk_running_sum(lf_ref[0], reverse=False)
    b_bw = _chunk_running_sum(lb_ref[0], reverse=True)

    for flag, pred in ((True, safe), (False, jnp.logical_not(safe))):
        @pl.when(pred)
        def _(flag=flag):
            _gla_step(qf_ref, kf_ref, vf_ref, of_ref, sf_ref, reverse=False, safe=flag, b_all=b_fw)
            _gla_step(qb_ref, kb_ref, vb_ref, ob_ref, sb_ref, reverse=True, safe=flag, b_all=b_bw)


def _gla(q, l_fw, k_fw, l_bw, k_bw, v, chunk):
    bsz, seq, d = q.shape
    nc = seq // chunk
    n_heads = d // HG_HEAD_DIM
    fw = pl.BlockSpec((1, chunk, d), lambda b, c: (b, c, 0))
    bw = pl.BlockSpec((1, chunk, d), lambda b, c: (b, nc - 1 - c, 0))
    state = pltpu.VMEM((n_heads, HG_HEAD_DIM, HG_HEAD_DIM), F32)
    return pl.pallas_call(
        _gla_kernel,
        grid=(bsz, nc),
        in_specs=[fw] * 4 + [bw] * 4,
        out_specs=[fw, bw],
        out_shape=[jax.ShapeDtypeStruct((bsz, seq, d), BF16)] * 2,
        scratch_shapes=[state, state],
        compiler_params=_cparams("parallel", "arbitrary"),
        name="hgrn_gla",
    )(q, l_fw, k_fw, v, q, l_bw, k_bw, v)


def _hgrn_out_kernel(of_ref, ob_ref, g_ref, h_ref, nw_ref, w_ref, out_ref, slab_ref):
    d = h_ref.shape[1]
    nw = nw_ref[...]
    parts = []
    for c0 in range(0, d, HG_HEAD_DIM):
        hs = slice(c0, c0 + HG_HEAD_DIM)
        o = of_ref[:, hs].astype(F32) + ob_ref[:, hs].astype(F32)
        y = _rmsnorm(o, nw) * g_ref[:, hs].astype(F32)
        parts.append(y.astype(BF16))
    y = jnp.concatenate(parts, axis=1)
    _deinterleave_add(h_ref, slab_ref, out_ref, _dot(y, w_ref[...]))


def _hgrn_out(o_fw, o_bw, g, h2, norm_w, w_out_bf, j, tm):
    t, d = h2.shape
    tile = pl.BlockSpec((tm, d), lambda i: (i, 0))
    return pl.pallas_call(
        _hgrn_out_kernel,
        grid=(t // tm,),
        in_specs=[tile, tile, tile, tile, _const_spec((1, HG_HEAD_DIM)), _layer_spec(w_out_bf, j)],
        out_specs=tile,
        out_shape=jax.ShapeDtypeStruct((t, d), F32),
        scratch_shapes=[pltpu.VMEM((d // LANES, tm, LANES), F32)],
        compiler_params=_cparams("parallel"),
        name="hgrn_out",
    )(o_fw, o_bw, g, h2, norm_w.reshape(1, HG_HEAD_DIM), w_out_bf)


def _halo_rows(hp_ref, hn_ref, nw):
    up = _rmsnorm(hp_ref[0], nw)
    un = _rmsnorm(hn_ref[0], nw)
    return jnp.concatenate([up, un], axis=0).astype(BF16)


def _halo_inside():
    i = pl.program_id(1)
    r = _iota2((2 * SUBLANES, 1), 0)
    has_prev = (i > 0).astype(F32)
    has_next = (i < pl.num_programs(1) - 1).astype(F32)
    return jnp.where(r < SUBLANES, has_prev, has_next) > 0.5


def _shifted(e_ref, x, xh, halo, n_rows):
    e_ref[0:SUBLANES, :] = xh[0:SUBLANES]
    e_ref[SUBLANES:SUBLANES + n_rows, :] = x
    e_ref[SUBLANES + n_rows:, :] = xh[SUBLANES:]
    return [x if j == halo else e_ref[pl.ds(SUBLANES - halo + j, n_rows), :] for j in range(2 * halo + 1)]


def _ffn_kernel(h_ref, hp_ref, hn_ref, nw_ref, wg_ref, wv_ref, cw_ref, cb_ref, wo_ref, fw_ref,
                out_ref, a_ref, e_ref, *, col_chunk, final_norm):
    tm = h_ref.shape[1]
    d_ff = wg_ref.shape[1]
    nw = nw_ref[...]
    hx = h_ref[0]
    u = _rmsnorm(hx, nw).astype(BF16)
    uh = _halo_rows(hp_ref, hn_ref, nw)
    inside = _halo_inside()
    for c0 in range(0, d_ff, col_chunk):
        cols = slice(c0, c0 + col_chunk)
        gate = _dot(u, wg_ref[:, cols])
        gh = jnp.where(inside, _dot(uh, wg_ref[:, cols]), 0.0)
        taps = _shifted(e_ref, gate, gh, FFN_CONV_HALO, tm)
        conv = cb_ref[:, cols]
        for j, tap in enumerate(taps):
            conv = conv + cw_ref[j:j + 1, cols] * tap
        val = _dot(u, wv_ref[:, cols])
        a_ref[:, cols] = (_silu(conv) * val).astype(BF16)
    y = hx + _dot(a_ref[...], wo_ref[...])
    if final_norm:
        y = _rmsnorm(y, fw_ref[...])
    out_ref[0] = y


def _halo_specs(tm, d, seq):
    nb = tm // SUBLANES
    n_blocks = seq // SUBLANES
    prev = pl.BlockSpec((1, SUBLANES, d), lambda b, i: (b, jnp.maximum(i * nb - 1, 0), 0))
    nxt = pl.BlockSpec((1, SUBLANES, d), lambda b, i: (b, jnp.minimum((i + 1) * nb, n_blocks - 1), 0))
    return prev, nxt


def _ffn(h3, norm_w, w_in_bf, conv_w, conv_b, wo_bf, layer, final_w, tm, final_norm):
    bsz, seq, d = h3.shape
    d_ff = wo_bf.shape[1]
    col_chunk = 256
    tile = pl.BlockSpec((1, tm, d), lambda b, i: (b, i, 0))
    prev, nxt = _halo_specs(tm, d, seq)
    return pl.pallas_call(
        functools.partial(_ffn_kernel, col_chunk=col_chunk, final_norm=final_norm),
        grid=(bsz, seq // tm),
        in_specs=[tile, prev, nxt, _const_spec((1, d)), _layer_spec(w_in_bf, layer, 2, 0),
                  _layer_spec(w_in_bf, layer, 2, 1), _const_spec(conv_w.shape), _const_spec((1, d_ff)),
                  _layer_spec(wo_bf, layer), _const_spec((1, d))],
        out_specs=tile,
        out_shape=jax.ShapeDtypeStruct(h3.shape, F32),
        scratch_shapes=[pltpu.VMEM((tm, d_ff), BF16), pltpu.VMEM((tm + 2 * SUBLANES, col_chunk), F32)],
        compiler_params=_cparams("parallel", "parallel"),
        name="convglu_final" if final_norm else "convglu",
    )(h3, h3, h3, norm_w.reshape(1, d), w_in_bf, w_in_bf, conv_w, conv_b.reshape(1, d_ff), wo_bf,
      final_w.reshape(1, d))


def _interleaved_taps(x, xh, halo):
    tm, w = x.shape
    nv = CHUNK // SUBLANES
    n_chunks = tm // CHUNK
    sub = _iota2((SUBLANES, w), 0)
    tile = lambda g, j: x[g * CHUNK + j * SUBLANES: g * CHUNK + (j + 1) * SUBLANES]

    def after(g, j):
        if g + 1 < n_chunks:
            return tile(g + 1, j)
        nxt = xh[SUBLANES:]
        return nxt if j == 0 else pltpu.roll(nxt, SUBLANES - j, 0)

    def before(g, j):
        if g > 0:
            return tile(g - 1, j)
        prv = xh[:SUBLANES]
        return prv if j == nv - 1 else pltpu.roll(prv, nv - 1 - j, 0)

    taps = [[] for _ in range(2 * halo + 1)]
    for g in range(n_chunks):
        lead = [pltpu.roll(jnp.where(sub == SUBLANES - 1, before(g, nv - halo + m), tile(g, nv - halo + m)), 1, 0)
                for m in range(halo)]
        tail = [pltpu.roll(jnp.where(sub == 0, after(g, m), tile(g, m)), SUBLANES - 1, 0) for m in range(halo)]
        ext = jnp.concatenate(lead + [x[g * CHUNK:(g + 1) * CHUNK]] + tail, axis=0)
        for j in range(2 * halo + 1):
            taps[j].append(ext[j * SUBLANES: j * SUBLANES + CHUNK])
    return [jnp.concatenate(t, axis=0) for t in taps]


def _ssm_proj_kernel(h_ref, hp_ref, hn_ref, nw_ref, wz_ref, wx_ref, wbc_ref, wdt_ref, cw_ref, cb_ref, dtb_ref,
                     z_ref, xs_ref, bm_ref, cm_ref, dt_ref, slab_ref, *, col_chunk):
    d_inner = z_ref.shape[2]
    gn = bm_ref.shape[2]
    nw = nw_ref[...]
    u = _rmsnorm(_interleave_rows(h_ref[0], slab_ref), nw).astype(BF16)
    uh = _halo_rows(hp_ref, hn_ref, nw)
    inside = _halo_inside()
    for c0 in range(0, d_inner, col_chunk):
        cols = slice(c0, c0 + col_chunk)
        z_ref[0, :, cols] = _silu(_dot(u, wz_ref[:, cols])).astype(BF16)
    for c0 in range(0, d_inner + 2 * gn, col_chunk):
        cols = slice(c0, c0 + col_chunk)
        w = wx_ref[:, cols] if c0 < d_inner else wbc_ref[:, c0 - d_inner: c0 - d_inner + col_chunk]
        x = _dot(u, w)
        xh = jnp.where(inside, _dot(uh, w), 0.0)
        taps = _interleaved_taps(x, xh, SSM_CONV_HALO)
        conv = cb_ref[:, cols]
        for j, tap in enumerate(taps):
            conv = conv + cw_ref[j:j + 1, cols] * tap
        act = _silu(conv).astype(BF16)
        if c0 < d_inner:
            xs_ref[0, :, cols] = act
        elif c0 < d_inner + gn:
            bm_ref[0, :, c0 - d_inner: c0 - d_inner + col_chunk] = act
        else:
            cm_ref[0, :, c0 - d_inner - gn: c0 - d_inner - gn + col_chunk] = act
    dt_ref[0] = _softplus(_dot(u, wdt_ref[...]) + dtb_ref[...])


def _ssm_proj(h3, norm_w, w_in_bf, j, wdt_bf, conv_w, conv_b, dt_bias_row, d_inner, tm):
    bsz, seq, d = h3.shape
    conv_dim = conv_w.shape[1]
    gn = (conv_dim - d_inner) // 2
    assert 2 * gn == d_inner
    col_chunk = 512
    tile = pl.BlockSpec((1, tm, d), lambda b, i: (b, i, 0))
    prev, nxt = _halo_specs(tm, d, seq)
    out_tile = lambda w: pl.BlockSpec((1, tm, w), lambda b, i: (b, i, 0))
    tok = lambda w, dt: jax.ShapeDtypeStruct((bsz, seq, w), dt)
    return pl.pallas_call(
        functools.partial(_ssm_proj_kernel, col_chunk=col_chunk),
        grid=(bsz, seq // tm),
        in_specs=[tile, prev, nxt, _const_spec((1, d)), _wide_layer_spec(w_in_bf, j, d_inner, 0),
                  _wide_layer_spec(w_in_bf, j, d_inner, 1), _wide_layer_spec(w_in_bf, j, d_inner, 2),
                  _const_spec(wdt_bf.shape), _const_spec(conv_w.shape), _const_spec((1, conv_dim)),
                  _const_spec((1, LANES))],
        out_specs=[out_tile(d_inner), out_tile(d_inner), out_tile(gn), out_tile(gn), out_tile(LANES)],
        out_shape=[tok(d_inner, BF16), tok(d_inner, BF16), tok(gn, BF16), tok(gn, BF16), tok(LANES, F32)],
        scratch_shapes=[pltpu.VMEM((d // LANES, tm, LANES), F32)],
        compiler_params=_cparams("parallel", "parallel"),
        name="ssm_proj",
    )(h3, h3, h3, norm_w.reshape(1, d), w_in_bf, w_in_bf, w_in_bf, wdt_bf, conv_w, conv_b.reshape(1, conv_dim),
      dt_bias_row)


def _ssd_step(xs_ref, bm_ref, cm_ref, dt_ref, alog_ref, y_ref, st_ref, *, reverse, skip_ref=None):
    c = xs_ref.shape[1]
    n_groups = st_ref.shape[0]
    n_state = st_ref.shape[1]
    gw = st_ref.shape[2]
    heads = n_groups * (gw // SSM_HEAD_DIM)
    h0 = heads if reverse else 0
    row = _chunk_token(_iota2((c, c), 0))
    col = _chunk_token(_iota2((c, c), 1))
    lane = _iota2((c, LANES), 1)
    keep = (row <= col) if reverse else (row >= col)
    last = 0 if reverse else c - 1

    dt = dt_ref[0]
    la = dt * (-LOG2E * jnp.exp(alog_ref[...]))
    a_col = _chunk_running_sum(la, reverse)
    a_row = a_col.T
    a_src = a_row - jnp.log(dt.T) * LOG2E
    w_row_bf = jnp.exp2(a_row[:, last:last + 1] - a_src).astype(BF16)

    hpg = gw // SSM_HEAD_DIM
    lane_g = _iota2((c, gw), 1)
    for g in range(n_groups):
        bg = bm_ref[0, :, g * n_state:(g + 1) * n_state]
        cg = cm_ref[0, :, g * n_state:(g + 1) * n_state]
        bt = bg.astype(F32).T.astype(BF16)
        st = st_ref[g]
        cg_out = _dot(cg, jnp.concatenate([bt, st.astype(BF16)], axis=1))
        cb = cg_out[:, :c].astype(BF16)
        y_in = cg_out[:, c:]
        xg = xs_ref[0, :, g * gw:(g + 1) * gw]
        zero = jnp.zeros_like(xg)
        ws, bws, bcs, x_blocks = [], [], [], []
        for j in range(hpg):
            hh = h0 + g * hpg + j
            bc = jnp.broadcast_to(a_col[:, hh:hh + 1], (c, c))
            decay = jnp.exp2(jnp.where(keep, bc - a_src[hh:hh + 1, :], -jnp.inf))
            ws.append(cb * decay.astype(BF16))
            bws.append(bt * w_row_bf[hh:hh + 1, :])
            bcs.append(bc)
            x_blocks.append(jnp.where((lane_g >= j * SSM_HEAD_DIM) & (lane_g < (j + 1) * SSM_HEAD_DIM), xg, zero))
        lhs = jnp.concatenate([jnp.concatenate(ws, axis=1), jnp.concatenate(bws, axis=1)], axis=0)
        res = _dot(lhs, jnp.concatenate(x_blocks, axis=0))
        e_parts = []
        for p in range(gw // LANES):
            e_parts.append(jnp.exp2(jnp.where(lane < SSM_HEAD_DIM, bcs[2 * p], bcs[2 * p + 1])))
        e_t = jnp.concatenate(e_parts, axis=1)
        y = res[:c] + y_in * e_t
        if skip_ref is not None:
            y = y + xg.astype(F32) * skip_ref[:, g * gw:(g + 1) * gw]
        y_ref[0, :, g * gw:(g + 1) * gw] = y.astype(y_ref.dtype)
        st_ref[g] = st * e_t[last:last + 1, :] + res[c:]


def _ssd_kernel(xf_ref, bf_ref, cf_ref, dtf_ref, xb_ref, bb_ref, cb_ref, dtb_ref, alog_ref, dsk_ref,
                yf_ref, yb_ref, sf_ref, sb_ref):
    @pl.when(pl.program_id(1) == 0)
    def _():
        sf_ref[...] = jnp.zeros_like(sf_ref)
        sb_ref[...] = jnp.zeros_like(sb_ref)

    _ssd_step(xf_ref, bf_ref, cf_ref, dtf_ref, alog_ref, yf_ref, sf_ref, reverse=False, skip_ref=dsk_ref)
    _ssd_step(xb_ref, bb_ref, cb_ref, dtb_ref, alog_ref, yb_ref, sb_ref, reverse=True)


def _ssd(xs, bm, cm, dt, alog_row, d_skip_row, chunk):
    bsz, seq, d_inner = xs.shape
    gn = bm.shape[2]
    nc = seq // chunk
    n_state = gn // SSM_GROUPS
    fw = lambda w: pl.BlockSpec((1, chunk, w), lambda b, c: (b, c, 0))
    bw = lambda w: pl.BlockSpec((1, chunk, w), lambda b, c: (b, nc - 1 - c, 0))
    state = pltpu.VMEM((SSM_GROUPS, n_state, d_inner // SSM_GROUPS), F32)
    return pl.pallas_call(
        _ssd_kernel,
        grid=(bsz, nc),
        in_specs=[fw(d_inner), fw(gn), fw(gn), fw(LANES), bw(d_inner), bw(gn), bw(gn), bw(LANES),
                  _const_spec((1, LANES)), _const_spec((1, d_inner))],
        out_specs=[fw(d_inner), bw(d_inner)],
        out_shape=[jax.ShapeDtypeStruct(xs.shape, BF16)] * 2,
        scratch_shapes=[state, state],
        compiler_params=_cparams("parallel", "arbitrary"),
        name="ssm_ssd",
    )(xs, bm, cm, dt, xs, bm, cm, dt, alog_row, d_skip_row)


def _ssm_out_kernel(yf_ref, yb_ref, z_ref, h_ref, nw_ref, w_ref, out_ref, slab_ref, *, group_w):
    d_inner = z_ref.shape[1]
    parts = []
    for c0 in range(0, d_inner, group_w):
        cs = slice(c0, c0 + group_w)
        y = (yf_ref[:, cs].astype(F32) + yb_ref[:, cs].astype(F32)) * z_ref[:, cs].astype(F32)
        parts.append(_rmsnorm(y, nw_ref[:, cs]).astype(BF16))
    y = jnp.concatenate(parts, axis=1)
    _deinterleave_add(h_ref, slab_ref, out_ref, _dot(y, w_ref[...]))


def _ssm_out(y_fw, y_bw, z, h2, norm_w, w_out_bf, j, tm):
    t, d = h2.shape
    d_inner = z.shape[1]
    wide = pl.BlockSpec((tm, d_inner), lambda i: (i, 0))
    tile = pl.BlockSpec((tm, d), lambda i: (i, 0))
    return pl.pallas_call(
        functools.partial(_ssm_out_kernel, group_w=d_inner // SSM_GROUPS),
        grid=(t // tm,),
        in_specs=[wide, wide, wide, tile, _const_spec((1, d_inner)), _layer_spec(w_out_bf, j)],
        out_specs=tile,
        out_shape=jax.ShapeDtypeStruct((t, d), F32),
        scratch_shapes=[pltpu.VMEM((d // LANES, tm, LANES), F32)],
        compiler_params=_cparams("parallel"),
        name="ssm_out",
    )(y_fw, y_bw, z, h2, norm_w.reshape(1, d_inner), w_out_bf)


def kernel(x, norm1_w, norm2_w, a_w_in, a_lb_logits, a_norm_w, a_w_out, b_w_in, b_conv_w, b_conv_b,
           b_dt_bias, b_a_log, b_d_skip, b_norm_w, b_w_out, ffn_w_in, ffn_conv_w, ffn_conv_b, ffn_w_out,
           final_norm_w):
    bsz, seq, d = x.shape
    depth = norm1_w.shape[0]
    t = bsz * seq
    tm = min(512, seq)
    d_inner = b_norm_w.shape[1]
    n_ssm_heads = b_dt_bias.shape[2]
    assert seq % tm == 0 and tm % CHUNK == 0
    assert 2 * n_ssm_heads <= LANES and d_inner == n_ssm_heads * SSM_HEAD_DIM

    bf = lambda w: w.astype(BF16)
    a_w_in, a_w_out, b_w_out, ffn_w_in, ffn_w_out = map(bf, (a_w_in, a_w_out, b_w_out, ffn_w_in, ffn_w_out))
    n_zxbc = d_inner + b_conv_w.shape[2]
    b_w_dt = bf(b_w_in[:, :, n_zxbc:])
    b_w_in = bf(b_w_in[:, :, :n_zxbc])
    h = x
    for i in range(depth):
        j = i // 2
        if i % 2 == 0:
            q, l_fw, k_fw, l_bw, k_bw, v, g = _hgrn_proj(h.reshape(t, d), norm1_w[i], a_w_in, j, a_lb_logits, i, tm)
            r3 = lambda a: a.reshape(bsz, seq, d)
            o_fw, o_bw = _gla(r3(q), r3(l_fw), r3(k_fw), r3(l_bw), r3(k_bw), r3(v), CHUNK)
            h = _hgrn_out(o_fw.reshape(t, d), o_bw.reshape(t, d), g, h.reshape(t, d), a_norm_w[j], a_w_out, j,
                          tm).reshape(bsz, seq, d)
        else:
            pad = LANES - 2 * n_ssm_heads
            wdt = jnp.pad(b_w_dt[j], ((0, 0), (0, pad)))
            dt_bias_row = jnp.pad(b_dt_bias[j].reshape(1, -1), ((0, 0), (0, pad)))
            alog_row = jnp.pad(b_a_log[j].reshape(1, -1), ((0, 0), (0, pad)))
            z, xs, bm, cm, dt = _ssm_proj(h, norm1_w[i], b_w_in, j, wdt, b_conv_w[j], b_conv_b[j], dt_bias_row,
                                          d_inner, tm)
            d_skip_row = jnp.repeat(b_d_skip[j], SSM_HEAD_DIM).reshape(1, d_inner)
            y_fw, y_bw = _ssd(xs, bm, cm, dt, alog_row, d_skip_row, CHUNK)
            f2 = lambda a: a.reshape(t, d_inner)
            h = _ssm_out(f2(y_fw), f2(y_bw), f2(z), h.reshape(t, d), b_norm_w[j], b_w_out, j,
                         tm).reshape(bsz, seq, d)
        h = _ffn(h, norm2_w[i], ffn_w_in, ffn_conv_w[i], ffn_conv_b[i], ffn_w_out, i, final_norm_w,
                 min(2 * tm, seq), final_norm=(i == depth - 1))
    return h
```

```python
import functools

import jax
import jax.numpy as jnp
from jax import lax
from jax.experimental import pallas as pl
from jax.experimental.pallas import tpu as pltpu

F32 = jnp.float32
BF16 = jnp.bfloat16

EPS = 1e-6
HG_HEAD_DIM = 128
SSM_HEAD_DIM = 64
SSM_GROUPS = 8
SSM_CONV_HALO = 2
FFN_CONV_HALO = 1
SUBLANES = 8
LANES = 128
CHUNK = 128
CHUNK_STREAM = CHUNK // SUBLANES
LOG2E = 1.4426950408889634
GLA_SAFE_LOG_DECAY = 75.0
GLA_STEP_CHUNKS = 2
VMEM_LIMIT_BYTES = 56 * 1024 * 1024


def _cparams(*sem):
    return pltpu.CompilerParams(dimension_semantics=sem, vmem_limit_bytes=VMEM_LIMIT_BYTES)


def _const_spec(shape, index=None):
    index = (0,) * len(shape) if index is None else index
    return pl.BlockSpec(shape, lambda *_: index, pipeline_mode=pl.Buffered(1))


def _layer_spec(w, layer, n_col_blocks=1, col_block=0):
    _, k, n = w.shape
    return pl.BlockSpec((None, k, n // n_col_blocks), lambda *_: (layer, 0, col_block),
                        pipeline_mode=pl.Buffered(1))


def _wide_layer_spec(w, layer, width, col_block):
    return pl.BlockSpec((None, w.shape[1], width), lambda *_: (layer, 0, col_block), pipeline_mode=pl.Buffered(1))


def _rmsnorm(x, w):
    return x * lax.rsqrt(jnp.mean(x * x, axis=-1, keepdims=True) + EPS) * w


def _sigmoid(x):
    return 0.5 * jnp.tanh(0.5 * x) + 0.5


def _silu_of_twice(h):
    return h * jnp.tanh(h) + h


def _silu(x):
    return _silu_of_twice(0.5 * x)


def _softplus(x):
    return jnp.maximum(x, 0.0) + jnp.log(1.0 + jnp.exp(-jnp.abs(x)))


def _dot(a, b):
    return jnp.dot(a, b, preferred_element_type=F32)


def _dot_nt(a, b):
    return lax.dot_general(a, b, (((1,), (1,)), ((), ())), preferred_element_type=F32)


def _dot_tn(a, b):
    return lax.dot_general(a, b, (((0,), (0,)), ((), ())), preferred_element_type=F32)


def _split3(x):
    hi = x.astype(BF16)
    r1 = x - hi.astype(F32)
    mid = r1.astype(BF16)
    lo = (r1 - mid.astype(F32)).astype(BF16)
    return hi, mid, lo


def _sel_rows(m01, x):
    hi, mid, lo = _split3(x)
    return _dot(m01, hi) + _dot(m01, mid) + _dot(m01, lo)


def _iota2(shape, dim):
    return lax.broadcasted_iota(jnp.int32, shape, dim)


def _chunk_token(idx):
    return (idx & (SUBLANES - 1)) * CHUNK_STREAM + (idx >> 3)


def _chunk_row(tok):
    return (tok % CHUNK_STREAM) * SUBLANES + tok // CHUNK_STREAM


def _chunk_running_sum(x, reverse):
    n = CHUNK // SUBLANES
    order = range(n - 1, -1, -1) if reverse else range(n)
    tiles, acc = [None] * n, None
    for i in order:
        t = x[i * SUBLANES:(i + 1) * SUBLANES]
        acc = t if acc is None else acc + t
        tiles[i] = acc
    sub = _iota2(acc.shape, 0)
    scan = acc
    for k in (1, 2, 4):
        if reverse:
            scan = scan + jnp.where(sub < SUBLANES - k, pltpu.roll(scan, SUBLANES - k, 0), 0.0)
        else:
            scan = scan + jnp.where(sub >= k, pltpu.roll(scan, k, 0), 0.0)
    offset = scan - acc
    return jnp.concatenate([t + offset for t in tiles], axis=0)


def _interleave_rows(x, slab_ref):
    tm, d = x.shape
    for s in range(d // LANES):
        for g in range(0, tm, CHUNK):
            for r in range(SUBLANES):
                slab_ref[s, pl.ds(g + r, CHUNK_STREAM, stride=SUBLANES), :] = (
                    x[g + r * CHUNK_STREAM: g + (r + 1) * CHUNK_STREAM, s * LANES:(s + 1) * LANES])
    return jnp.concatenate([slab_ref[s] for s in range(d // LANES)], axis=1)


def _deinterleave_add(res_ref, slab_ref, out_ref, y):
    tm, d = y.shape
    for s in range(d // LANES):
        slab_ref[s] = y[:, s * LANES:(s + 1) * LANES]
    for s in range(d // LANES):
        for g in range(0, tm, CHUNK):
            for r in range(SUBLANES):
                rows = slice(g + r * CHUNK_STREAM, g + (r + 1) * CHUNK_STREAM)
                lanes = slice(s * LANES, (s + 1) * LANES)
                out_ref[rows, lanes] = res_ref[rows, lanes] + slab_ref[s, pl.ds(g + r, CHUNK_STREAM, stride=SUBLANES), :]


def _hgrn_proj_kernel(h_ref, nw_ref, w_ref, lbl_ref, q_ref, lf_ref, kf_ref, lb_ref, kb_ref, v_ref, g_ref,
                      slab_ref, *, layer, col_chunk):
    d = h_ref.shape[1]
    u = _rmsnorm(_interleave_rows(h_ref[...], slab_ref), nw_ref[...]).astype(BF16)
    lg = lbl_ref[...]
    n_rows = lg.shape[0]
    mx = lg[0:1]
    for r in range(1, n_rows):
        mx = jnp.maximum(mx, lg[r:r + 1])
    es = [jnp.exp(lg[r:r + 1] - mx) for r in range(n_rows)]
    tot = es[0]
    for r in range(1, n_rows):
        tot = tot + es[r]
    part = es[0]
    for r in range(1, layer + 1):
        part = part + es[r]
    lb = part / tot

    def gate(fr, cols, l_ref, k_ref):
        lbc = lb[:, cols]
        f = lbc + (1.0 - lbc) * _sigmoid(fr)
        l_ref[:, cols] = jnp.log2(f)
        k_ref[:, cols] = (1.0 - f).astype(BF16)

    for sec in range(5):
        for c0 in range(0, d, col_chunk):
            cols = slice(c0, c0 + col_chunk)
            p = _dot(u, w_ref[:, sec * d + c0: sec * d + c0 + col_chunk])
            if sec == 0:
                q_ref[:, cols] = _silu(p).astype(BF16)
            elif sec == 1:
                gate(p, cols, lf_ref, kf_ref)
            elif sec == 2:
                gate(p, cols, lb_ref, kb_ref)
            elif sec == 3:
                v_ref[:, cols] = p.astype(BF16)
            else:
                g_ref[:, cols] = _silu(p).astype(BF16)


def _hgrn_proj(h2, norm_w, w_in_bf, j, lb_logits, layer, tm):
    t, d = h2.shape
    tile = pl.BlockSpec((tm, d), lambda i: (i, 0))
    tok = lambda dt: jax.ShapeDtypeStruct((t, d), dt)
    return pl.pallas_call(
        functools.partial(_hgrn_proj_kernel, layer=layer, col_chunk=min(512, d)),
        grid=(t // tm,),
        in_specs=[tile, _const_spec((1, d)), _layer_spec(w_in_bf, j), _const_spec(lb_logits.shape)],
        out_specs=[tile] * 7,
        out_shape=[tok(BF16), tok(F32), tok(BF16), tok(F32), tok(BF16), tok(BF16), tok(BF16)],
        scratch_shapes=[pltpu.VMEM((d // LANES, tm, LANES), F32)],
        compiler_params=_cparams("parallel"),
        name="hgrn_proj",
    )(h2, norm_w.reshape(1, d), w_in_bf, lb_logits)


def _gla_step(q_ref, k_ref, v_ref, o_ref, st_ref, *, rows, reverse, safe, b_all):
    c = CHUNK
    n_heads = st_ref.shape[0]
    hd = st_ref.shape[2]
    row = _chunk_token(_iota2((c, c), 0))
    col = _chunk_token(_iota2((c, c), 1))
    last = 0 if reverse else c - 1
    mid = _chunk_row(c // 2 if reverse else c // 2 - 1)
    for h in range(n_heads):
        hs = slice(h * hd, (h + 1) * hd)
        b = b_all[:, hs]
        bl = b[last:last + 1, :]
        vb = v_ref[0, rows, hs]
        st = st_ref[h]
        if safe:
            bm = b[mid:mid + 1, :]
            dm = b - bm
            q_hat = q_ref[0, rows, hs] * jnp.exp2(dm).astype(BF16)
            k_hat = k_ref[0, rows, hs] * jnp.exp2(-dm).astype(BF16)
            keep = (row <= col) if reverse else (row >= col)
            qk = _dot_nt(q_hat, jnp.concatenate([k_hat, (st * jnp.exp2(bm)).astype(BF16)], axis=0))
            att = jnp.where(keep, qk[:, :c], 0.0)
            o = qk[:, c:] + _dot(att.astype(BF16), vb)
            kv = _dot_tn(vb, k_hat) * jnp.exp2(bl - bm)
        else:
            qf = q_ref[0, rows, hs].astype(F32)
            kf = k_ref[0, rows, hs].astype(F32)
            att = jnp.where(row == col, _dot_nt(q_ref[0, rows, hs], k_ref[0, rows, hs]), 0.0)
            lvl = 0
            while (1 << lvl) < c:
                bs = 1 << lvl
                edge = (row >> (lvl + 1)) * (2 * bs) + (bs if reverse else bs - 1)
                g = _sel_rows((col == edge).astype(BF16), b)
                q_l = (qf * jnp.exp2(jnp.minimum(b - g, 0.0))).astype(BF16)
                k_l = (kf * jnp.exp2(jnp.minimum(g - b, 0.0))).astype(BF16)
                same_parent = (row >> (lvl + 1)) == (col >> (lvl + 1))
                if reverse:
                    pair = same_parent & ((col >> lvl) == (row >> lvl) + 1)
                else:
                    pair = same_parent & ((row >> lvl) == (col >> lvl) + 1)
                att = att + jnp.where(pair, _dot_nt(q_l, k_l), 0.0)
                lvl += 1
            q_in = (qf * jnp.exp2(b)).astype(BF16)
            o = _dot_nt(q_in, st.astype(BF16)) + _dot(att.astype(BF16), vb)
            kv = _dot_tn(vb, (kf * jnp.exp2(bl - b)).astype(BF16))
        o_ref[0, rows, hs] = o.astype(o_ref.dtype)
        st_ref[h] = st * jnp.exp2(bl) + kv


def _gla_kernel(qf_ref, lf_ref, kf_ref, vf_ref, qb_ref, lb_ref, kb_ref, vb_ref, of_ref, ob_ref, sf_ref, sb_ref):
    @pl.when(pl.program_id(1) == 0)
    def _():
        sf_ref[...] = jnp.zeros_like(sf_ref)
        sb_ref[...] = jnp.zeros_like(sb_ref)

    def half_chunk_decay(l_ref, rows):
        tot = l_ref[0, rows.start:rows.start + SUBLANES, :]
        for i in range(1, CHUNK // SUBLANES):
            tot = tot + l_ref[0, rows.start + i * SUBLANES:rows.start + (i + 1) * SUBLANES, :]
        s = tot + pltpu.roll(tot, 1, 0)
        s = s + pltpu.roll(s, 2, 0)
        lo, hi = SUBLANES // 2 - 1, SUBLANES - 1
        return jnp.minimum(s[lo:lo + 1, :], s[hi:hi + 1, :])

    n_sub = qf_ref.shape[1] // CHUNK
    for i in range(n_sub):
        rf = slice(i * CHUNK, (i + 1) * CHUNK)
        rb = slice((n_sub - 1 - i) * CHUNK, (n_sub - i) * CHUNK)
        worst = jnp.min(jnp.minimum(half_chunk_decay(lf_ref, rf), half_chunk_decay(lb_ref, rb)))
        safe = worst >= -GLA_SAFE_LOG_DECAY * LOG2E
        b_fw = _chunk_running_sum(lf_ref[0, rf, :], reverse=False)
        b_bw = _chunk_running_sum(lb_ref[0, rb, :], reverse=True)

        for flag, pred in ((True, safe), (False, jnp.logical_not(safe))):
            @pl.when(pred)
            def _(flag=flag, rf=rf, rb=rb, b_fw=b_fw, b_bw=b_bw):
                _gla_step(qf_ref, kf_ref, vf_ref, of_ref, sf_ref, rows=rf, reverse=False, safe=flag, b_all=b_fw)
                _gla_step(qb_ref, kb_ref, vb_ref, ob_ref, sb_ref, rows=rb, reverse=True, safe=flag, b_all=b_bw)


def _gla(q, l_fw, k_fw, l_bw, k_bw, v, chunk):
    bsz, seq, d = q.shape
    nc = seq // chunk
    n_heads = d // HG_HEAD_DIM
    fw = pl.BlockSpec((1, chunk, d), lambda b, c: (b, c, 0))
    bw = pl.BlockSpec((1, chunk, d), lambda b, c: (b, nc - 1 - c, 0))
    state = pltpu.VMEM((n_heads, HG_HEAD_DIM, HG_HEAD_DIM), F32)
    return pl.pallas_call(
        _gla_kernel,
        grid=(bsz, nc),
        in_specs=[fw] * 4 + [bw] * 4,
        out_specs=[fw, bw],
        out_shape=[jax.ShapeDtypeStruct((bsz, seq, d), BF16)] * 2,
        scratch_shapes=[state, state],
        compiler_params=_cparams("parallel", "arbitrary"),
        name="hgrn_gla",
    )(q, l_fw, k_fw, v, q, l_bw, k_bw, v)


def _hgrn_out_kernel(of_ref, ob_ref, g_ref, h_ref, nw_ref, w_ref, out_ref, slab_ref):
    d = h_ref.shape[1]
    nw = nw_ref[...]
    parts = []
    for c0 in range(0, d, HG_HEAD_DIM):
        hs = slice(c0, c0 + HG_HEAD_DIM)
        o = of_ref[:, hs].astype(F32) + ob_ref[:, hs].astype(F32)
        y = _rmsnorm(o, nw) * g_ref[:, hs].astype(F32)
        parts.append(y.astype(BF16))
    y = jnp.concatenate(parts, axis=1)
    _deinterleave_add(h_ref, slab_ref, out_ref, _dot(y, w_ref[...]))


def _hgrn_out(o_fw, o_bw, g, h2, norm_w, w_out_bf, j, tm):
    t, d = h2.shape
    tile = pl.BlockSpec((tm, d), lambda i: (i, 0))
    return pl.pallas_call(
        _hgrn_out_kernel,
        grid=(t // tm,),
        in_specs=[tile, tile, tile, tile, _const_spec((1, HG_HEAD_DIM)), _layer_spec(w_out_bf, j)],
        out_specs=tile,
        out_shape=jax.ShapeDtypeStruct((t, d), F32),
        scratch_shapes=[pltpu.VMEM((d // LANES, tm, LANES), F32)],
        compiler_params=_cparams("parallel"),
        name="hgrn_out",
    )(o_fw, o_bw, g, h2, norm_w.reshape(1, HG_HEAD_DIM), w_out_bf)


def _halo_rows(hp_ref, hn_ref, nw):
    up = _rmsnorm(hp_ref[0], nw)
    un = _rmsnorm(hn_ref[0], nw)
    return jnp.concatenate([up, un], axis=0).astype(BF16)


def _halo_inside():
    i = pl.program_id(1)
    r = _iota2((2 * SUBLANES, 1), 0)
    has_prev = (i > 0).astype(F32)
    has_next = (i < pl.num_programs(1) - 1).astype(F32)
    return jnp.where(r < SUBLANES, has_prev, has_next) > 0.5


def _shifted(e_ref, x, xh, halo, n_rows):
    e_ref[0:SUBLANES, :] = xh[0:SUBLANES]
    e_ref[SUBLANES:SUBLANES + n_rows, :] = x
    e_ref[SUBLANES + n_rows:, :] = xh[SUBLANES:]
    return [x if j == halo else e_ref[pl.ds(SUBLANES - halo + j, n_rows), :] for j in range(2 * halo + 1)]


def _ffn_kernel(h_ref, hp_ref, hn_ref, nw_ref, wg_ref, wv_ref, cw_ref, cb_ref, wo_ref, fw_ref,
                out_ref, a_ref, e_ref, *, col_chunk, final_norm):
    tm = h_ref.shape[1]
    d_ff = wg_ref.shape[1]
    nw = nw_ref[...]
    hx = h_ref[0]
    u = _rmsnorm(hx, nw).astype(BF16)
    uh = _halo_rows(hp_ref, hn_ref, nw)
    inside = _halo_inside()
    for c0 in range(0, d_ff, col_chunk):
        cols = slice(c0, c0 + col_chunk)
        gate = _dot(u, wg_ref[:, cols])
        gh = jnp.where(inside, _dot(uh, wg_ref[:, cols]), 0.0)
        taps = _shifted(e_ref, gate, gh, FFN_CONV_HALO, tm)
        half_conv = 0.5 * cb_ref[:, cols]
        for j, tap in enumerate(taps):
            half_conv = half_conv + (0.5 * cw_ref[j:j + 1, cols]) * tap
        val = _dot(u, wv_ref[:, cols])
        a_ref[:, cols] = (_silu_of_twice(half_conv) * val).astype(BF16)
    y = hx + _dot(a_ref[...], wo_ref[...])
    if final_norm:
        y = _rmsnorm(y, fw_ref[...])
    out_ref[0] = y


def _halo_specs(tm, d, seq):
    nb = tm // SUBLANES
    n_blocks = seq // SUBLANES
    prev = pl.BlockSpec((1, SUBLANES, d), lambda b, i: (b, jnp.maximum(i * nb - 1, 0), 0))
    nxt = pl.BlockSpec((1, SUBLANES, d), lambda b, i: (b, jnp.minimum((i + 1) * nb, n_blocks - 1), 0))
    return prev, nxt


def _ffn(h3, norm_w, w_in_bf, conv_w, conv_b, wo_bf, layer, final_w, tm, final_norm):
    bsz, seq, d = h3.shape
    d_ff = wo_bf.shape[1]
    col_chunk = 256
    tile = pl.BlockSpec((1, tm, d), lambda b, i: (b, i, 0))
    prev, nxt = _halo_specs(tm, d, seq)
    return pl.pallas_call(
        functools.partial(_ffn_kernel, col_chunk=col_chunk, final_norm=final_norm),
        grid=(bsz, seq // tm),
        in_specs=[tile, prev, nxt, _const_spec((1, d)), _layer_spec(w_in_bf, layer, 2, 0),
                  _layer_spec(w_in_bf, layer, 2, 1), _const_spec(conv_w.shape), _const_spec((1, d_ff)),
                  _layer_spec(wo_bf, layer), _const_spec((1, d))],
        out_specs=tile,
        out_shape=jax.ShapeDtypeStruct(h3.shape, F32),
        scratch_shapes=[pltpu.VMEM((tm, d_ff), BF16), pltpu.VMEM((tm + 2 * SUBLANES, col_chunk), F32)],
        compiler_params=_cparams("parallel", "parallel"),
        name="convglu_final" if final_norm else "convglu",
    )(h3, h3, h3, norm_w.reshape(1, d), w_in_bf, w_in_bf, conv_w, conv_b.reshape(1, d_ff), wo_bf,
      final_w.reshape(1, d))


def _interleaved_taps(x, xh, halo):
    tm, w = x.shape
    nv = CHUNK // SUBLANES
    n_chunks = tm // CHUNK
    sub = _iota2((SUBLANES, w), 0)
    tile = lambda g, j: x[g * CHUNK + j * SUBLANES: g * CHUNK + (j + 1) * SUBLANES]

    def after(g, j):
        if g + 1 < n_chunks:
            return tile(g + 1, j)
        nxt = xh[SUBLANES:]
        return nxt if j == 0 else pltpu.roll(nxt, SUBLANES - j, 0)

    def before(g, j):
        if g > 0:
            return tile(g - 1, j)
        prv = xh[:SUBLANES]
        return prv if j == nv - 1 else pltpu.roll(prv, nv - 1 - j, 0)

    taps = [[] for _ in range(2 * halo + 1)]
    for g in range(n_chunks):
        lead = [pltpu.roll(jnp.where(sub == SUBLANES - 1, before(g, nv - halo + m), tile(g, nv - halo + m)), 1, 0)
                for m in range(halo)]
        tail = [pltpu.roll(jnp.where(sub == 0, after(g, m), tile(g, m)), SUBLANES - 1, 0) for m in range(halo)]
        ext = jnp.concatenate(lead + [x[g * CHUNK:(g + 1) * CHUNK]] + tail, axis=0)
        for j in range(2 * halo + 1):
            taps[j].append(ext[j * SUBLANES: j * SUBLANES + CHUNK])
    return [jnp.concatenate(t, axis=0) for t in taps]


def _ssm_proj_kernel(h_ref, hp_ref, hn_ref, nw_ref, wz_ref, wx_ref, wbc_ref, wdt_ref, cw_ref, cb_ref, dtb_ref,
                     z_ref, xs_ref, bm_ref, cm_ref, dt_ref, slab_ref, *, col_chunk):
    d_inner = z_ref.shape[2]
    gn = bm_ref.shape[2]
    nw = nw_ref[...]
    u = _rmsnorm(_interleave_rows(h_ref[0], slab_ref), nw).astype(BF16)
    uh = _halo_rows(hp_ref, hn_ref, nw)
    inside = _halo_inside()
    for c0 in range(0, d_inner, col_chunk):
        cols = slice(c0, c0 + col_chunk)
        z_ref[0, :, cols] = _silu(_dot(u, wz_ref[:, cols])).astype(BF16)
    for c0 in range(0, d_inner + 2 * gn, col_chunk):
        cols = slice(c0, c0 + col_chunk)
        w = wx_ref[:, cols] if c0 < d_inner else wbc_ref[:, c0 - d_inner: c0 - d_inner + col_chunk]
        x = _dot(u, w)
        xh = jnp.where(inside, _dot(uh, w), 0.0)
        taps = _interleaved_taps(x, xh, SSM_CONV_HALO)
        half_conv = 0.5 * cb_ref[:, cols]
        for j, tap in enumerate(taps):
            half_conv = half_conv + (0.5 * cw_ref[j:j + 1, cols]) * tap
        act = _silu_of_twice(half_conv).astype(BF16)
        if c0 < d_inner:
            xs_ref[0, :, cols] = act
        elif c0 < d_inner + gn:
            bm_ref[0, :, c0 - d_inner: c0 - d_inner + col_chunk] = act
        else:
            cm_ref[0, :, c0 - d_inner - gn: c0 - d_inner - gn + col_chunk] = act
    dt_ref[0] = _softplus(_dot(u, wdt_ref[...]) + dtb_ref[...])


def _ssm_proj(h3, norm_w, w_in_bf, j, wdt_bf, conv_w, conv_b, dt_bias_row, d_inner, tm):
    bsz, seq, d = h3.shape
    conv_dim = conv_w.shape[1]
    gn = (conv_dim - d_inner) // 2
    assert 2 * gn == d_inner
    col_chunk = 512
    tile = pl.BlockSpec((1, tm, d), lambda b, i: (b, i, 0))
    prev, nxt = _halo_specs(tm, d, seq)
    out_tile = lambda w: pl.BlockSpec((1, tm, w), lambda b, i: (b, i, 0))
    tok = lambda w, dt: jax.ShapeDtypeStruct((bsz, seq, w), dt)
    return pl.pallas_call(
        functools.partial(_ssm_proj_kernel, col_chunk=col_chunk),
        grid=(bsz, seq // tm),
        in_specs=[tile, prev, nxt, _const_spec((1, d)), _wide_layer_spec(w_in_bf, j, d_inner, 0),
                  _wide_layer_spec(w_in_bf, j, d_inner, 1), _wide_layer_spec(w_in_bf, j, d_inner, 2),
                  _const_spec(wdt_bf.shape), _const_spec(conv_w.shape), _const_spec((1, conv_dim)),
                  _const_spec((1, LANES))],
        out_specs=[out_tile(d_inner), out_tile(d_inner), out_tile(gn), out_tile(gn), out_tile(LANES)],
        out_shape=[tok(d_inner, BF16), tok(d_inner, BF16), tok(gn, BF16), tok(gn, BF16), tok(LANES, F32)],
        scratch_shapes=[pltpu.VMEM((d // LANES, tm, LANES), F32)],
        compiler_params=_cparams("parallel", "parallel"),
        name="ssm_proj",
    )(h3, h3, h3, norm_w.reshape(1, d), w_in_bf, w_in_bf, w_in_bf, wdt_bf, conv_w, conv_b.reshape(1, conv_dim),
      dt_bias_row)


def _ssd_step(xs_ref, bm_ref, cm_ref, dt_ref, alog_ref, y_ref, st_ref, *, reverse, skip_ref=None):
    c = xs_ref.shape[1]
    n_groups = st_ref.shape[0]
    n_state = st_ref.shape[1]
    gw = st_ref.shape[2]
    heads = n_groups * (gw // SSM_HEAD_DIM)
    h0 = heads if reverse else 0
    row = _chunk_token(_iota2((c, c), 0))
    col = _chunk_token(_iota2((c, c), 1))
    lane = _iota2((c, LANES), 1)
    keep = (row <= col) if reverse else (row >= col)
    last = 0 if reverse else c - 1

    dt = dt_ref[0]
    la = dt * (-LOG2E * jnp.exp(alog_ref[...]))
    a_col = _chunk_running_sum(la, reverse)
    a_row = a_col.T
    a_src = a_row - jnp.log(dt.T) * LOG2E
    w_row_bf = jnp.exp2(a_row[:, last:last + 1] - a_src).astype(BF16)

    hpg = gw // SSM_HEAD_DIM
    lane_g = _iota2((c, gw), 1)
    for g in range(n_groups):
        bg = bm_ref[0, :, g * n_state:(g + 1) * n_state]
        cg = cm_ref[0, :, g * n_state:(g + 1) * n_state]
        bt = bg.astype(F32).T.astype(BF16)
        st = st_ref[g]
        cg_out = _dot(cg, jnp.concatenate([bt, st.astype(BF16)], axis=1))
        cb = cg_out[:, :c].astype(BF16)
        y_in = cg_out[:, c:]
        xg = xs_ref[0, :, g * gw:(g + 1) * gw]
        zero = jnp.zeros_like(xg)
        ws, bws, bcs, x_blocks = [], [], [], []
        for j in range(hpg):
            hh = h0 + g * hpg + j
            bc = jnp.broadcast_to(a_col[:, hh:hh + 1], (c, c))
            decay = jnp.exp2(jnp.where(keep, bc - a_src[hh:hh + 1, :], -jnp.inf))
            ws.append(cb * decay.astype(BF16))
            bws.append(bt * w_row_bf[hh:hh + 1, :])
            bcs.append(bc)
            x_blocks.append(jnp.where((lane_g >= j * SSM_HEAD_DIM) & (lane_g < (j + 1) * SSM_HEAD_DIM), xg, zero))
        lhs = jnp.concatenate([jnp.concatenate(ws, axis=1), jnp.concatenate(bws, axis=1)], axis=0)
        res = _dot(lhs, jnp.concatenate(x_blocks, axis=0))
        e_parts = []
        for p in range(gw // LANES):
            e_parts.append(jnp.exp2(jnp.where(lane < SSM_HEAD_DIM, bcs[2 * p], bcs[2 * p + 1])))
        e_t = jnp.concatenate(e_parts, axis=1)
        y = res[:c] + y_in * e_t
        if skip_ref is not None:
            y = y + xg.astype(F32) * skip_ref[:, g * gw:(g + 1) * gw]
        y_ref[0, :, g * gw:(g + 1) * gw] = y.astype(y_ref.dtype)
        st_ref[g] = st * e_t[last:last + 1, :] + res[c:]


def _ssd_kernel(xf_ref, bf_ref, cf_ref, dtf_ref, xb_ref, bb_ref, cb_ref, dtb_ref, alog_ref, dsk_ref,
                yf_ref, yb_ref, sf_ref, sb_ref):
    @pl.when(pl.program_id(1) == 0)
    def _():
        sf_ref[...] = jnp.zeros_like(sf_ref)
        sb_ref[...] = jnp.zeros_like(sb_ref)

    _ssd_step(xf_ref, bf_ref, cf_ref, dtf_ref, alog_ref, yf_ref, sf_ref, reverse=False, skip_ref=dsk_ref)
    _ssd_step(xb_ref, bb_ref, cb_ref, dtb_ref, alog_ref, yb_ref, sb_ref, reverse=True)


def _ssd(xs, bm, cm, dt, alog_row, d_skip_row, chunk):
    bsz, seq, d_inner = xs.shape
    gn = bm.shape[2]
    nc = seq // chunk
    n_state = gn // SSM_GROUPS
    fw = lambda w: pl.BlockSpec((1, chunk, w), lambda b, c: (b, c, 0))
    bw = lambda w: pl.BlockSpec((1, chunk, w), lambda b, c: (b, nc - 1 - c, 0))
    state = pltpu.VMEM((SSM_GROUPS, n_state, d_inner // SSM_GROUPS), F32)
    return pl.pallas_call(
        _ssd_kernel,
        grid=(bsz, nc),
        in_specs=[fw(d_inner), fw(gn), fw(gn), fw(LANES), bw(d_inner), bw(gn), bw(gn), bw(LANES),
                  _const_spec((1, LANES)), _const_spec((1, d_inner))],
        out_specs=[fw(d_inner), bw(d_inner)],
        out_shape=[jax.ShapeDtypeStruct(xs.shape, BF16)] * 2,
        scratch_shapes=[state, state],
        compiler_params=_cparams("parallel", "arbitrary"),
        name="ssm_ssd",
    )(xs, bm, cm, dt, xs, bm, cm, dt, alog_row, d_skip_row)


def _ssm_out_kernel(yf_ref, yb_ref, z_ref, h_ref, nw_ref, w_ref, out_ref, slab_ref, *, group_w):
    d_inner = z_ref.shape[1]
    parts = []
    for c0 in range(0, d_inner, group_w):
        cs = slice(c0, c0 + group_w)
        y = (yf_ref[:, cs].astype(F32) + yb_ref[:, cs].astype(F32)) * z_ref[:, cs].astype(F32)
        parts.append(_rmsnorm(y, nw_ref[:, cs]).astype(BF16))
    y = jnp.concatenate(parts, axis=1)
    _deinterleave_add(h_ref, slab_ref, out_ref, _dot(y, w_ref[...]))


def _ssm_out(y_fw, y_bw, z, h2, norm_w, w_out_bf, j, tm):
    t, d = h2.shape
    d_inner = z.shape[1]
    wide = pl.BlockSpec((tm, d_inner), lambda i: (i, 0))
    tile = pl.BlockSpec((tm, d), lambda i: (i, 0))
    return pl.pallas_call(
        functools.partial(_ssm_out_kernel, group_w=d_inner // SSM_GROUPS),
        grid=(t // tm,),
        in_specs=[wide, wide, wide, tile, _const_spec((1, d_inner)), _layer_spec(w_out_bf, j)],
        out_specs=tile,
        out_shape=jax.ShapeDtypeStruct((t, d), F32),
        scratch_shapes=[pltpu.VMEM((d // LANES, tm, LANES), F32)],
        compiler_params=_cparams("parallel"),
        name="ssm_out",
    )(y_fw, y_bw, z, h2, norm_w.reshape(1, d_inner), w_out_bf)


def kernel(x, norm1_w, norm2_w, a_w_in, a_lb_logits, a_norm_w, a_w_out, b_w_in, b_conv_w, b_conv_b,
           b_dt_bias, b_a_log, b_d_skip, b_norm_w, b_w_out, ffn_w_in, ffn_conv_w, ffn_conv_b, ffn_w_out,
           final_norm_w):
    bsz, seq, d = x.shape
    depth = norm1_w.shape[0]
    t = bsz * seq
    tm = min(512, seq)
    d_inner = b_norm_w.shape[1]
    n_ssm_heads = b_dt_bias.shape[2]
    assert seq % tm == 0 and tm % CHUNK == 0 and seq % (GLA_STEP_CHUNKS * CHUNK) == 0
    assert 2 * n_ssm_heads <= LANES and d_inner == n_ssm_heads * SSM_HEAD_DIM

    bf = lambda w: w.astype(BF16)
    a_w_in, a_w_out, b_w_out, ffn_w_in, ffn_w_out = map(bf, (a_w_in, a_w_out, b_w_out, ffn_w_in, ffn_w_out))
    n_zxbc = d_inner + b_conv_w.shape[2]
    b_w_dt = bf(b_w_in[:, :, n_zxbc:])
    b_w_in = bf(lax.optimization_barrier(b_w_in[:, :, :n_zxbc]))
    h = x
    for i in range(depth):
        j = i // 2
        if i % 2 == 0:
            q, l_fw, k_fw, l_bw, k_bw, v, g = _hgrn_proj(h.reshape(t, d), norm1_w[i], a_w_in, j, a_lb_logits, i, tm)
            r3 = lambda a: a.reshape(bsz, seq, d)
            o_fw, o_bw = _gla(r3(q), r3(l_fw), r3(k_fw), r3(l_bw), r3(k_bw), r3(v), GLA_STEP_CHUNKS * CHUNK)
            h = _hgrn_out(o_fw.reshape(t, d), o_bw.reshape(t, d), g, h.reshape(t, d), a_norm_w[j], a_w_out, j,
                          tm).reshape(bsz, seq, d)
        else:
            pad = LANES - 2 * n_ssm_heads
            wdt = jnp.pad(b_w_dt[j], ((0, 0), (0, pad)))
            dt_bias_row = jnp.pad(b_dt_bias[j].reshape(1, -1), ((0, 0), (0, pad)))
            alog_row = jnp.pad(b_a_log[j].reshape(1, -1), ((0, 0), (0, pad)))
            z, xs, bm, cm, dt = _ssm_proj(h, norm1_w[i], b_w_in, j, wdt, b_conv_w[j], b_conv_b[j], dt_bias_row,
                                          d_inner, tm)
            d_skip_row = jnp.repeat(b_d_skip[j], SSM_HEAD_DIM).reshape(1, d_inner)
            y_fw, y_bw = _ssd(xs, bm, cm, dt, alog_row, d_skip_row, CHUNK)
            f2 = lambda a: a.reshape(t, d_inner)
            h = _ssm_out(f2(y_fw), f2(y_bw), f2(z), h.reshape(t, d), b_norm_w[j], b_w_out, j,
                         tm).reshape(bsz, seq, d)
        h = _ffn(h, norm2_w[i], ffn_w_in, ffn_conv_w[i], ffn_conv_b[i], ffn_w_out, i, final_norm_w,
                 min(2 * tm, seq), final_norm=(i == depth - 1))
    return h
```

```python
import functools

import jax
import jax.numpy as jnp
from jax import lax
from jax.experimental import pallas as pl
from jax.experimental.pallas import tpu as pltpu

F32 = jnp.float32
BF16 = jnp.bfloat16

EPS = 1e-6
HG_HEAD_DIM = 128
SSM_HEAD_DIM = 64
SSM_GROUPS = 8
SSM_CONV_HALO = 2
FFN_CONV_HALO = 1
SUBLANES = 8
LANES = 128
CHUNK = 128
CHUNK_STREAM = CHUNK // SUBLANES
LOG2E = 1.4426950408889634
GLA_SAFE_LOG_DECAY = 75.0
GLA_STEP_CHUNKS = 2
VMEM_LIMIT_BYTES = 56 * 1024 * 1024


def _cparams(*sem):
    return pltpu.CompilerParams(dimension_semantics=sem, vmem_limit_bytes=VMEM_LIMIT_BYTES)


def _const_spec(shape, index=None):
    index = (0,) * len(shape) if index is None else index
    return pl.BlockSpec(shape, lambda *_: index, pipeline_mode=pl.Buffered(1))


def _layer_spec(w, layer, n_col_blocks=1, col_block=0):
    _, k, n = w.shape
    return pl.BlockSpec((None, k, n // n_col_blocks), lambda *_: (layer, 0, col_block),
                        pipeline_mode=pl.Buffered(1))


def _wide_layer_spec(w, layer, width, col_block):
    return pl.BlockSpec((None, w.shape[1], width), lambda *_: (layer, 0, col_block), pipeline_mode=pl.Buffered(1))


def _rmsnorm(x, w):
    return x * lax.rsqrt(jnp.mean(x * x, axis=-1, keepdims=True) + EPS) * w


def _sigmoid(x):
    return 0.5 * jnp.tanh(0.5 * x) + 0.5


def _silu_of_twice(h):
    return h * jnp.tanh(h) + h


def _silu(x):
    return _silu_of_twice(0.5 * x)


def _softplus(x):
    return jnp.maximum(x, 0.0) + jnp.log(1.0 + jnp.exp(-jnp.abs(x)))


def _dot(a, b):
    return jnp.dot(a, b, preferred_element_type=F32)


def _dot_nt(a, b):
    return lax.dot_general(a, b, (((1,), (1,)), ((), ())), preferred_element_type=F32)


def _dot_tn(a, b):
    return lax.dot_general(a, b, (((0,), (0,)), ((), ())), preferred_element_type=F32)


def _split3(x):
    hi = x.astype(BF16)
    r1 = x - hi.astype(F32)
    mid = r1.astype(BF16)
    lo = (r1 - mid.astype(F32)).astype(BF16)
    return hi, mid, lo


def _sel_rows(m01, x):
    hi, mid, lo = _split3(x)
    return _dot(m01, hi) + _dot(m01, mid) + _dot(m01, lo)


def _iota2(shape, dim):
    return lax.broadcasted_iota(jnp.int32, shape, dim)


def _chunk_token(idx):
    return (idx & (SUBLANES - 1)) * CHUNK_STREAM + (idx >> 3)


def _chunk_row(tok):
    return (tok % CHUNK_STREAM) * SUBLANES + tok // CHUNK_STREAM


def _chunk_running_sum(x, reverse):
    n = CHUNK // SUBLANES
    order = range(n - 1, -1, -1) if reverse else range(n)
    tiles, acc = [None] * n, None
    for i in order:
        t = x[i * SUBLANES:(i + 1) * SUBLANES]
        acc = t if acc is None else acc + t
        tiles[i] = acc
    sub = _iota2(acc.shape, 0)
    scan = acc
    for k in (1, 2, 4):
        if reverse:
            scan = scan + jnp.where(sub < SUBLANES - k, pltpu.roll(scan, SUBLANES - k, 0), 0.0)
        else:
            scan = scan + jnp.where(sub >= k, pltpu.roll(scan, k, 0), 0.0)
    offset = scan - acc
    return jnp.concatenate([t + offset for t in tiles], axis=0)


def _interleave_rows(x, slab_ref):
    tm, d = x.shape
    for s in range(d // LANES):
        for g in range(0, tm, CHUNK):
            for r in range(SUBLANES):
                slab_ref[s, pl.ds(g + r, CHUNK_STREAM, stride=SUBLANES), :] = (
                    x[g + r * CHUNK_STREAM: g + (r + 1) * CHUNK_STREAM, s * LANES:(s + 1) * LANES])
    return jnp.concatenate([slab_ref[s] for s in range(d // LANES)], axis=1)


def _deinterleave_add(res_ref, slab_ref, out_ref, y):
    tm, d = y.shape
    for s in range(d // LANES):
        slab_ref[s] = y[:, s * LANES:(s + 1) * LANES]
    for s in range(d // LANES):
        for g in range(0, tm, CHUNK):
            for r in range(SUBLANES):
                rows = slice(g + r * CHUNK_STREAM, g + (r + 1) * CHUNK_STREAM)
                lanes = slice(s * LANES, (s + 1) * LANES)
                out_ref[rows, lanes] = res_ref[rows, lanes] + slab_ref[s, pl.ds(g + r, CHUNK_STREAM, stride=SUBLANES), :]


def _hgrn_proj_kernel(h_ref, nw_ref, w_ref, lbl_ref, q_ref, lf_ref, kf_ref, lb_ref, kb_ref, v_ref, g_ref,
                      slab_ref, *, layer, col_chunk):
    d = h_ref.shape[1]
    u = _rmsnorm(_interleave_rows(h_ref[...], slab_ref), nw_ref[...]).astype(BF16)
    lg = lbl_ref[...]
    n_rows = lg.shape[0]
    mx = lg[0:1]
    for r in range(1, n_rows):
        mx = jnp.maximum(mx, lg[r:r + 1])
    es = [jnp.exp(lg[r:r + 1] - mx) for r in range(n_rows)]
    tot = es[0]
    for r in range(1, n_rows):
        tot = tot + es[r]
    part = es[0]
    for r in range(1, layer + 1):
        part = part + es[r]
    lb = part / tot

    def gate(fr, cols, l_ref, k_ref):
        lbc = lb[:, cols]
        f = lbc + (1.0 - lbc) * _sigmoid(fr)
        l_ref[:, cols] = jnp.log2(f)
        k_ref[:, cols] = (1.0 - f).astype(BF16)

    for sec in range(5):
        for c0 in range(0, d, col_chunk):
            cols = slice(c0, c0 + col_chunk)
            p = _dot(u, w_ref[:, sec * d + c0: sec * d + c0 + col_chunk])
            if sec == 0:
                q_ref[:, cols] = _silu(p).astype(BF16)
            elif sec == 1:
                gate(p, cols, lf_ref, kf_ref)
            elif sec == 2:
                gate(p, cols, lb_ref, kb_ref)
            elif sec == 3:
                v_ref[:, cols] = p.astype(BF16)
            else:
                g_ref[:, cols] = _silu(p).astype(BF16)


def _hgrn_proj(h2, norm_w, w_in_bf, j, lb_logits, layer, tm):
    t, d = h2.shape
    tile = pl.BlockSpec((tm, d), lambda i: (i, 0))
    tok = lambda dt: jax.ShapeDtypeStruct((t, d), dt)
    return pl.pallas_call(
        functools.partial(_hgrn_proj_kernel, layer=layer, col_chunk=min(512, d)),
        grid=(t // tm,),
        in_specs=[tile, _const_spec((1, d)), _layer_spec(w_in_bf, j), _const_spec(lb_logits.shape)],
        out_specs=[tile] * 7,
        out_shape=[tok(BF16), tok(F32), tok(BF16), tok(F32), tok(BF16), tok(BF16), tok(BF16)],
        scratch_shapes=[pltpu.VMEM((d // LANES, tm, LANES), F32)],
        compiler_params=_cparams("parallel"),
        name="hgrn_proj",
    )(h2, norm_w.reshape(1, d), w_in_bf, lb_logits)


def _gla_step(q_ref, k_ref, v_ref, o_ref, st_ref, *, rows, reverse, safe, b_all):
    c = CHUNK
    n_heads = st_ref.shape[0]
    hd = st_ref.shape[2]
    row = _chunk_token(_iota2((c, c), 0))
    col = _chunk_token(_iota2((c, c), 1))
    last = 0 if reverse else c - 1
    mid = _chunk_row(c // 2 if reverse else c // 2 - 1)
    for h in range(n_heads):
        hs = slice(h * hd, (h + 1) * hd)
        b = b_all[:, hs]
        bl = b[last:last + 1, :]
        vb = v_ref[0, rows, hs]
        st = st_ref[h]
        if safe:
            bm = b[mid:mid + 1, :]
            dm = b - bm
            q_hat = q_ref[0, rows, hs] * jnp.exp2(dm).astype(BF16)
            k_hat = k_ref[0, rows, hs] * jnp.exp2(-dm).astype(BF16)
            keep = (row <= col) if reverse else (row >= col)
            qk = _dot_nt(q_hat, jnp.concatenate([k_hat, (st * jnp.exp2(bm)).astype(BF16)], axis=0))
            att = jnp.where(keep, qk[:, :c], 0.0)
            o = qk[:, c:] + _dot(att.astype(BF16), vb)
            kv = _dot_tn(vb, k_hat) * jnp.exp2(bl - bm)
        else:
            qf = q_ref[0, rows, hs].astype(F32)
            kf = k_ref[0, rows, hs].astype(F32)
            att = jnp.where(row == col, _dot_nt(q_ref[0, rows, hs], k_ref[0, rows, hs]), 0.0)
            lvl = 0
            while (1 << lvl) < c:
                bs = 1 << lvl
                edge = (row >> (lvl + 1)) * (2 * bs) + (bs if reverse else bs - 1)
                g = _sel_rows((col == edge).astype(BF16), b)
                q_l = (qf * jnp.exp2(jnp.minimum(b - g, 0.0))).astype(BF16)
                k_l = (kf * jnp.exp2(jnp.minimum(g - b, 0.0))).astype(BF16)
                same_parent = (row >> (lvl + 1)) == (col >> (lvl + 1))
                if reverse:
                    pair = same_parent & ((col >> lvl) == (row >> lvl) + 1)
                else:
                    pair = same_parent & ((row >> lvl) == (col >> lvl) + 1)
                att = att + jnp.where(pair, _dot_nt(q_l, k_l), 0.0)
                lvl += 1
            q_in = (qf * jnp.exp2(b)).astype(BF16)
            o = _dot_nt(q_in, st.astype(BF16)) + _dot(att.astype(BF16), vb)
            kv = _dot_tn(vb, (kf * jnp.exp2(bl - b)).astype(BF16))
        o_ref[0, rows, hs] = o.astype(o_ref.dtype)
        st_ref[h] = st * jnp.exp2(bl) + kv


def _gla_kernel(qf_ref, lf_ref, kf_ref, vf_ref, qb_ref, lb_ref, kb_ref, vb_ref, of_ref, ob_ref, sf_ref, sb_ref):
    @pl.when(pl.program_id(1) == 0)
    def _():
        sf_ref[...] = jnp.zeros_like(sf_ref)
        sb_ref[...] = jnp.zeros_like(sb_ref)

    def half_chunk_decay(l_ref, rows):
        tot = l_ref[0, rows.start:rows.start + SUBLANES, :]
        for i in range(1, CHUNK // SUBLANES):
            tot = tot + l_ref[0, rows.start + i * SUBLANES:rows.start + (i + 1) * SUBLANES, :]
        s = tot + pltpu.roll(tot, 1, 0)
        s = s + pltpu.roll(s, 2, 0)
        lo, hi = SUBLANES // 2 - 1, SUBLANES - 1
        return jnp.minimum(s[lo:lo + 1, :], s[hi:hi + 1, :])

    n_sub = qf_ref.shape[1] // CHUNK
    for i in range(n_sub):
        rf = slice(i * CHUNK, (i + 1) * CHUNK)
        rb = slice((n_sub - 1 - i) * CHUNK, (n_sub - i) * CHUNK)
        worst = jnp.min(jnp.minimum(half_chunk_decay(lf_ref, rf), half_chunk_decay(lb_ref, rb)))
        safe = worst >= -GLA_SAFE_LOG_DECAY * LOG2E
        b_fw = _chunk_running_sum(lf_ref[0, rf, :], reverse=False)
        b_bw = _chunk_running_sum(lb_ref[0, rb, :], reverse=True)

        for flag, pred in ((True, safe), (False, jnp.logical_not(safe))):
            @pl.when(pred)
            def _(flag=flag, rf=rf, rb=rb, b_fw=b_fw, b_bw=b_bw):
                _gla_step(qf_ref, kf_ref, vf_ref, of_ref, sf_ref, rows=rf, reverse=False, safe=flag, b_all=b_fw)
                _gla_step(qb_ref, kb_ref, vb_ref, ob_ref, sb_ref, rows=rb, reverse=True, safe=flag, b_all=b_bw)


def _gla(q, l_fw, k_fw, l_bw, k_bw, v, chunk):
    bsz, seq, d = q.shape
    nc = seq // chunk
    n_heads = d // HG_HEAD_DIM
    fw = pl.BlockSpec((1, chunk, d), lambda b, c: (b, c, 0))
    bw = pl.BlockSpec((1, chunk, d), lambda b, c: (b, nc - 1 - c, 0))
    state = pltpu.VMEM((n_heads, HG_HEAD_DIM, HG_HEAD_DIM), F32)
    return pl.pallas_call(
        _gla_kernel,
        grid=(bsz, nc),
        in_specs=[fw] * 4 + [bw] * 4,
        out_specs=[fw, bw],
        out_shape=[jax.ShapeDtypeStruct((bsz, seq, d), BF16)] * 2,
        scratch_shapes=[state, state],
        compiler_params=_cparams("parallel", "arbitrary"),
        name="hgrn_gla",
    )(q, l_fw, k_fw, v, q, l_bw, k_bw, v)


def _hgrn_out_kernel(of_ref, ob_ref, g_ref, h_ref, nw_ref, w_ref, out_ref, slab_ref):
    d = h_ref.shape[1]
    nw = nw_ref[...]
    parts = []
    for c0 in range(0, d, HG_HEAD_DIM):
        hs = slice(c0, c0 + HG_HEAD_DIM)
        o = of_ref[:, hs].astype(F32) + ob_ref[:, hs].astype(F32)
        y = _rmsnorm(o, nw) * g_ref[:, hs].astype(F32)
        parts.append(y.astype(BF16))
    y = jnp.concatenate(parts, axis=1)
    _deinterleave_add(h_ref, slab_ref, out_ref, _dot(y, w_ref[...]))


def _hgrn_out(o_fw, o_bw, g, h2, norm_w, w_out_bf, j, tm):
    t, d = h2.shape
    tile = pl.BlockSpec((tm, d), lambda i: (i, 0))
    return pl.pallas_call(
        _hgrn_out_kernel,
        grid=(t // tm,),
        in_specs=[tile, tile, tile, tile, _const_spec((1, HG_HEAD_DIM)), _layer_spec(w_out_bf, j)],
        out_specs=tile,
        out_shape=jax.ShapeDtypeStruct((t, d), F32),
        scratch_shapes=[pltpu.VMEM((d // LANES, tm, LANES), F32)],
        compiler_params=_cparams("parallel"),
        name="hgrn_out",
    )(o_fw, o_bw, g, h2, norm_w.reshape(1, HG_HEAD_DIM), w_out_bf)


def _halo_rows(hp_ref, hn_ref, nw):
    up = _rmsnorm(hp_ref[0], nw)
    un = _rmsnorm(hn_ref[0], nw)
    return jnp.concatenate([up, un], axis=0).astype(BF16)


def _halo_inside():
    i = pl.program_id(1)
    r = _iota2((2 * SUBLANES, 1), 0)
    has_prev = (i > 0).astype(F32)
    has_next = (i < pl.num_programs(1) - 1).astype(F32)
    return jnp.where(r < SUBLANES, has_prev, has_next) > 0.5


def _shifted(e_ref, x, xh, halo, n_rows):
    e_ref[0:SUBLANES, :] = xh[0:SUBLANES]
    e_ref[SUBLANES:SUBLANES + n_rows, :] = x
    e_ref[SUBLANES + n_rows:, :] = xh[SUBLANES:]
    return [x if j == halo else e_ref[pl.ds(SUBLANES - halo + j, n_rows), :] for j in range(2 * halo + 1)]


def _ffn_kernel(h_ref, hp_ref, hn_ref, nw_ref, wg_ref, wv_ref, cw_ref, cb_ref, wo_ref, fw_ref,
                out_ref, a_ref, e_ref, *, col_chunk, final_norm):
    tm = h_ref.shape[1]
    d_ff = wg_ref.shape[1]
    nw = nw_ref[...]
    hx = h_ref[0]
    u = _rmsnorm(hx, nw).astype(BF16)
    uh = _halo_rows(hp_ref, hn_ref, nw)
    inside = _halo_inside()
    for c0 in range(0, d_ff, col_chunk):
        cols = slice(c0, c0 + col_chunk)
        gate = _dot(u, wg_ref[:, cols])
        gh = jnp.where(inside, _dot(uh, wg_ref[:, cols]), 0.0)
        taps = _shifted(e_ref, gate, gh, FFN_CONV_HALO, tm)
        half_conv = 0.5 * cb_ref[:, cols]
        for j, tap in enumerate(taps):
            half_conv = half_conv + (0.5 * cw_ref[j:j + 1, cols]) * tap
        val = _dot(u, wv_ref[:, cols])
        a_ref[:, cols] = (_silu_of_twice(half_conv) * val).astype(BF16)
    y = hx + _dot(a_ref[...], wo_ref[...])
    if final_norm:
        y = _rmsnorm(y, fw_ref[...])
    out_ref[0] = y


def _halo_specs(tm, d, seq):
    nb = tm // SUBLANES
    n_blocks = seq // SUBLANES
    prev = pl.BlockSpec((1, SUBLANES, d), lambda b, i: (b, jnp.maximum(i * nb - 1, 0), 0))
    nxt = pl.BlockSpec((1, SUBLANES, d), lambda b, i: (b, jnp.minimum((i + 1) * nb, n_blocks - 1), 0))
    return prev, nxt


def _ffn(h3, norm_w, w_in_bf, conv_w, conv_b, wo_bf, layer, final_w, tm, final_norm):
    bsz, seq, d = h3.shape
    d_ff = wo_bf.shape[1]
    col_chunk = 256
    tile = pl.BlockSpec((1, tm, d), lambda b, i: (b, i, 0))
    prev, nxt = _halo_specs(tm, d, seq)
    return pl.pallas_call(
        functools.partial(_ffn_kernel, col_chunk=col_chunk, final_norm=final_norm),
        grid=(bsz, seq // tm),
        in_specs=[tile, prev, nxt, _const_spec((1, d)), _layer_spec(w_in_bf, layer, 2, 0),
                  _layer_spec(w_in_bf, layer, 2, 1), _const_spec(conv_w.shape), _const_spec((1, d_ff)),
                  _layer_spec(wo_bf, layer), _const_spec((1, d))],
        out_specs=tile,
        out_shape=jax.ShapeDtypeStruct(h3.shape, F32),
        scratch_shapes=[pltpu.VMEM((tm, d_ff), BF16), pltpu.VMEM((tm + 2 * SUBLANES, col_chunk), F32)],
        compiler_params=_cparams("parallel", "parallel"),
        name="convglu_final" if final_norm else "convglu",
    )(h3, h3, h3, norm_w.reshape(1, d), w_in_bf, w_in_bf, conv_w, conv_b.reshape(1, d_ff), wo_bf,
      final_w.reshape(1, d))


def _interleaved_taps(x, xh, halo):
    tm, w = x.shape
    nv = CHUNK // SUBLANES
    n_chunks = tm // CHUNK
    sub = _iota2((SUBLANES, w), 0)
    tile = lambda g, j: x[g * CHUNK + j * SUBLANES: g * CHUNK + (j + 1) * SUBLANES]

    def after(g, j):
        if g + 1 < n_chunks:
            return tile(g + 1, j)
        nxt = xh[SUBLANES:]
        return nxt if j == 0 else pltpu.roll(nxt, SUBLANES - j, 0)

    def before(g, j):
        if g > 0:
            return tile(g - 1, j)
        prv = xh[:SUBLANES]
        return prv if j == nv - 1 else pltpu.roll(prv, nv - 1 - j, 0)

    taps = [[] for _ in range(2 * halo + 1)]
    for g in range(n_chunks):
        lead = [pltpu.roll(jnp.where(sub == SUBLANES - 1, before(g, nv - halo + m), tile(g, nv - halo + m)), 1, 0)
                for m in range(halo)]
        tail = [pltpu.roll(jnp.where(sub == 0, after(g, m), tile(g, m)), SUBLANES - 1, 0) for m in range(halo)]
        ext = jnp.concatenate(lead + [x[g * CHUNK:(g + 1) * CHUNK]] + tail, axis=0)
        for j in range(2 * halo + 1):
            taps[j].append(ext[j * SUBLANES: j * SUBLANES + CHUNK])
    return [jnp.concatenate(t, axis=0) for t in taps]


def _ssm_proj_kernel(h_ref, hp_ref, hn_ref, nw_ref, wz_ref, wx_ref, wbc_ref, wdt_ref, cw_ref, cb_ref, dtb_ref,
                     z_ref, xs_ref, bm_ref, cm_ref, dt_ref, slab_ref, *, col_chunk):
    d_inner = z_ref.shape[2]
    gn = bm_ref.shape[2]
    nw = nw_ref[...]
    u = _rmsnorm(_interleave_rows(h_ref[0], slab_ref), nw).astype(BF16)
    uh = _halo_rows(hp_ref, hn_ref, nw)
    inside = _halo_inside()
    for c0 in range(0, d_inner, col_chunk):
        cols = slice(c0, c0 + col_chunk)
        z_ref[0, :, cols] = _silu(_dot(u, wz_ref[:, cols])).astype(BF16)
    for c0 in range(0, d_inner + 2 * gn, col_chunk):
        cols = slice(c0, c0 + col_chunk)
        w = wx_ref[:, cols] if c0 < d_inner else wbc_ref[:, c0 - d_inner: c0 - d_inner + col_chunk]
        x = _dot(u, w)
        xh = jnp.where(inside, _dot(uh, w), 0.0)
        taps = _interleaved_taps(x, xh, SSM_CONV_HALO)
        half_conv = 0.5 * cb_ref[:, cols]
        for j, tap in enumerate(taps):
            half_conv = half_conv + (0.5 * cw_ref[j:j + 1, cols]) * tap
        act = _silu_of_twice(half_conv).astype(BF16)
        if c0 < d_inner:
            xs_ref[0, :, cols] = act
        elif c0 < d_inner + gn:
            bm_ref[0, :, c0 - d_inner: c0 - d_inner + col_chunk] = act
        else:
            cm_ref[0, :, c0 - d_inner - gn: c0 - d_inner - gn + col_chunk] = act
    dt_ref[0] = _softplus(_dot(u, wdt_ref[...]) + dtb_ref[...])


def _ssm_proj(h3, norm_w, w_in_bf, j, wdt_bf, conv_w, conv_b, dt_bias_row, d_inner, tm):
    bsz, seq, d = h3.shape
    conv_dim = conv_w.shape[1]
    gn = (conv_dim - d_inner) // 2
    assert 2 * gn == d_inner
    col_chunk = 512
    tile = pl.BlockSpec((1, tm, d), lambda b, i: (b, i, 0))
    prev, nxt = _halo_specs(tm, d, seq)
    out_tile = lambda w: pl.BlockSpec((1, tm, w), lambda b, i: (b, i, 0))
    tok = lambda w, dt: jax.ShapeDtypeStruct((bsz, seq, w), dt)
    return pl.pallas_call(
        functools.partial(_ssm_proj_kernel, col_chunk=col_chunk),
        grid=(bsz, seq // tm),
        in_specs=[tile, prev, nxt, _const_spec((1, d)), _wide_layer_spec(w_in_bf, j, d_inner, 0),
                  _wide_layer_spec(w_in_bf, j, d_inner, 1), _wide_layer_spec(w_in_bf, j, d_inner, 2),
                  _const_spec(wdt_bf.shape), _const_spec(conv_w.shape), _const_spec((1, conv_dim)),
                  _const_spec((1, LANES))],
        out_specs=[out_tile(d_inner), out_tile(d_inner), out_tile(gn), out_tile(gn), out_tile(LANES)],
        out_shape=[tok(d_inner, BF16), tok(d_inner, BF16), tok(gn, BF16), tok(gn, BF16), tok(LANES, F32)],
        scratch_shapes=[pltpu.VMEM((d // LANES, tm, LANES), F32)],
        compiler_params=_cparams("parallel", "parallel"),
        name="ssm_proj",
    )(h3, h3, h3, norm_w.reshape(1, d), w_in_bf, w_in_bf, w_in_bf, wdt_bf, conv_w, conv_b.reshape(1, conv_dim),
      dt_bias_row)


def _ssd_step(xs_ref, bm_ref, cm_ref, dt_ref, z_ref, alog_ref, y_ref, st_ref, *, reverse, skip_ref=None):
    c = xs_ref.shape[1]
    n_groups = st_ref.shape[0]
    n_state = st_ref.shape[1]
    gw = st_ref.shape[2]
    heads = n_groups * (gw // SSM_HEAD_DIM)
    h0 = heads if reverse else 0
    row = _chunk_token(_iota2((c, c), 0))
    col = _chunk_token(_iota2((c, c), 1))
    lane = _iota2((c, LANES), 1)
    keep = (row <= col) if reverse else (row >= col)
    last = 0 if reverse else c - 1

    dt = dt_ref[0]
    la = dt * (-LOG2E * jnp.exp(alog_ref[...]))
    a_col = _chunk_running_sum(la, reverse)
    a_row = a_col.T
    a_src = a_row - jnp.log(dt.T) * LOG2E
    w_row_bf = jnp.exp2(a_row[:, last:last + 1] - a_src).astype(BF16)

    hpg = gw // SSM_HEAD_DIM
    lane_g = _iota2((c, gw), 1)
    for g in range(n_groups):
        bg = bm_ref[0, :, g * n_state:(g + 1) * n_state]
        cg = cm_ref[0, :, g * n_state:(g + 1) * n_state]
        bt = bg.astype(F32).T.astype(BF16)
        st = st_ref[g]
        cg_out = _dot(cg, jnp.concatenate([bt, st.astype(BF16)], axis=1))
        cb = cg_out[:, :c].astype(BF16)
        y_in = cg_out[:, c:]
        xg = xs_ref[0, :, g * gw:(g + 1) * gw]
        zero = jnp.zeros_like(xg)
        ws, bws, bcs, x_blocks = [], [], [], []
        for j in range(hpg):
            hh = h0 + g * hpg + j
            bc = jnp.broadcast_to(a_col[:, hh:hh + 1], (c, c))
            decay = jnp.exp2(jnp.where(keep, bc - a_src[hh:hh + 1, :], -jnp.inf))
            ws.append(cb * decay.astype(BF16))
            bws.append(bt * w_row_bf[hh:hh + 1, :])
            bcs.append(bc)
            x_blocks.append(jnp.where((lane_g >= j * SSM_HEAD_DIM) & (lane_g < (j + 1) * SSM_HEAD_DIM), xg, zero))
        lhs = jnp.concatenate([jnp.concatenate(ws, axis=1), jnp.concatenate(bws, axis=1)], axis=0)
        res = _dot(lhs, jnp.concatenate(x_blocks, axis=0))
        e_parts = []
        for p in range(gw // LANES):
            e_parts.append(jnp.exp2(jnp.where(lane < SSM_HEAD_DIM, bcs[2 * p], bcs[2 * p + 1])))
        e_t = jnp.concatenate(e_parts, axis=1)
        y = res[:c] + y_in * e_t
        if skip_ref is not None:
            y = y + xg.astype(F32) * skip_ref[:, g * gw:(g + 1) * gw]
        y_ref[0, :, g * gw:(g + 1) * gw] = (y * z_ref[0, :, g * gw:(g + 1) * gw].astype(F32)).astype(y_ref.dtype)
        st_ref[g] = st * e_t[last:last + 1, :] + res[c:]


def _ssd_kernel(xf_ref, bf_ref, cf_ref, dtf_ref, zf_ref, xb_ref, bb_ref, cb_ref, dtb_ref, zb_ref, alog_ref, dsk_ref,
                yf_ref, yb_ref, sf_ref, sb_ref):
    @pl.when(pl.program_id(1) == 0)
    def _():
        sf_ref[...] = jnp.zeros_like(sf_ref)
        sb_ref[...] = jnp.zeros_like(sb_ref)

    _ssd_step(xf_ref, bf_ref, cf_ref, dtf_ref, zf_ref, alog_ref, yf_ref, sf_ref, reverse=False, skip_ref=dsk_ref)
    _ssd_step(xb_ref, bb_ref, cb_ref, dtb_ref, zb_ref, alog_ref, yb_ref, sb_ref, reverse=True)


def _ssd(xs, bm, cm, dt, z, alog_row, d_skip_row, chunk):
    bsz, seq, d_inner = xs.shape
    gn = bm.shape[2]
    nc = seq // chunk
    n_state = gn // SSM_GROUPS
    fw = lambda w: pl.BlockSpec((1, chunk, w), lambda b, c: (b, c, 0))
    bw = lambda w: pl.BlockSpec((1, chunk, w), lambda b, c: (b, nc - 1 - c, 0))
    state = pltpu.VMEM((SSM_GROUPS, n_state, d_inner // SSM_GROUPS), F32)
    return pl.pallas_call(
        _ssd_kernel,
        grid=(bsz, nc),
        in_specs=[fw(d_inner), fw(gn), fw(gn), fw(LANES), fw(d_inner), bw(d_inner), bw(gn), bw(gn), bw(LANES),
                  bw(d_inner), _const_spec((1, LANES)), _const_spec((1, d_inner))],
        out_specs=[fw(d_inner), bw(d_inner)],
        out_shape=[jax.ShapeDtypeStruct(xs.shape, BF16)] * 2,
        scratch_shapes=[state, state],
        compiler_params=_cparams("parallel", "arbitrary"),
        name="ssm_ssd",
    )(xs, bm, cm, dt, z, xs, bm, cm, dt, z, alog_row, d_skip_row)


def _ssm_out_kernel(yf_ref, yb_ref, h_ref, nw_ref, w_ref, out_ref, slab_ref, *, group_w):
    d_inner = yf_ref.shape[1]
    parts = []
    for c0 in range(0, d_inner, group_w):
        cs = slice(c0, c0 + group_w)
        y = yf_ref[:, cs].astype(F32) + yb_ref[:, cs].astype(F32)
        parts.append(_rmsnorm(y, nw_ref[:, cs]).astype(BF16))
    y = jnp.concatenate(parts, axis=1)
    _deinterleave_add(h_ref, slab_ref, out_ref, _dot(y, w_ref[...]))


def _ssm_out(y_fw, y_bw, h2, norm_w, w_out_bf, j, tm):
    t, d = h2.shape
    d_inner = y_fw.shape[1]
    wide = pl.BlockSpec((tm, d_inner), lambda i: (i, 0))
    tile = pl.BlockSpec((tm, d), lambda i: (i, 0))
    return pl.pallas_call(
        functools.partial(_ssm_out_kernel, group_w=d_inner // SSM_GROUPS),
        grid=(t // tm,),
        in_specs=[wide, wide, tile, _const_spec((1, d_inner)), _layer_spec(w_out_bf, j)],
        out_specs=tile,
        out_shape=jax.ShapeDtypeStruct((t, d), F32),
        scratch_shapes=[pltpu.VMEM((d // LANES, tm, LANES), F32)],
        compiler_params=_cparams("parallel"),
        name="ssm_out",
    )(y_fw, y_bw, h2, norm_w.reshape(1, d_inner), w_out_bf)


def kernel(x, norm1_w, norm2_w, a_w_in, a_lb_logits, a_norm_w, a_w_out, b_w_in, b_conv_w, b_conv_b,
           b_dt_bias, b_a_log, b_d_skip, b_norm_w, b_w_out, ffn_w_in, ffn_conv_w, ffn_conv_b, ffn_w_out,
           final_norm_w):
    bsz, seq, d = x.shape
    depth = norm1_w.shape[0]
    t = bsz * seq
    tm = min(512, seq)
    d_inner = b_norm_w.shape[1]
    n_ssm_heads = b_dt_bias.shape[2]
    assert seq % tm == 0 and tm % CHUNK == 0 and seq % (GLA_STEP_CHUNKS * CHUNK) == 0
    assert 2 * n_ssm_heads <= LANES and d_inner == n_ssm_heads * SSM_HEAD_DIM

    bf = lambda w: w.astype(BF16)
    a_w_in, a_w_out, b_w_out, ffn_w_in, ffn_w_out = map(bf, (a_w_in, a_w_out, b_w_out, ffn_w_in, ffn_w_out))
    n_zxbc = d_inner + b_conv_w.shape[2]
    b_w_dt = bf(b_w_in[:, :, n_zxbc:])
    b_w_in = bf(lax.optimization_barrier(b_w_in[:, :, :n_zxbc]))
    h = x
    for i in range(depth):
        j = i // 2
        if i % 2 == 0:
            q, l_fw, k_fw, l_bw, k_bw, v, g = _hgrn_proj(h.reshape(t, d), norm1_w[i], a_w_in, j, a_lb_logits, i, tm)
            r3 = lambda a: a.reshape(bsz, seq, d)
            o_fw, o_bw = _gla(r3(q), r3(l_fw), r3(k_fw), r3(l_bw), r3(k_bw), r3(v), GLA_STEP_CHUNKS * CHUNK)
            h = _hgrn_out(o_fw.reshape(t, d), o_bw.reshape(t, d), g, h.reshape(t, d), a_norm_w[j], a_w_out, j,
                          min(2 * tm, seq)).reshape(bsz, seq, d)
        else:
            pad = LANES - 2 * n_ssm_heads
            wdt = jnp.pad(b_w_dt[j], ((0, 0), (0, pad)))
            dt_bias_row = jnp.pad(b_dt_bias[j].reshape(1, -1), ((0, 0), (0, pad)))
            alog_row = jnp.pad(b_a_log[j].reshape(1, -1), ((0, 0), (0, pad)))
            z, xs, bm, cm, dt = _ssm_proj(h, norm1_w[i], b_w_in, j, wdt, b_conv_w[j], b_conv_b[j], dt_bias_row,
                                          d_inner, tm)
            d_skip_row = jnp.repeat(b_d_skip[j], SSM_HEAD_DIM).reshape(1, d_inner)
            y_fw, y_bw = _ssd(xs, bm, cm, dt, z, alog_row, d_skip_row, CHUNK)
            f2 = lambda a: a.reshape(t, d_inner)
            h = _ssm_out(f2(y_fw), f2(y_bw), h.reshape(t, d), b_norm_w[j], b_w_out, j, tm).reshape(bsz, seq, d)
        h = _ffn(h, norm2_w[i], ffn_w_in, ffn_conv_w[i], ffn_conv_b[i], ffn_w_out, i, final_norm_w,
                 min(2 * tm, seq), final_norm=(i == depth - 1))
    return h
```

```python
import functools

import jax
import jax.numpy as jnp
from jax import lax
from jax.experimental import pallas as pl
from jax.experimental.pallas import tpu as pltpu

F32 = jnp.float32
BF16 = jnp.bfloat16

EPS = 1e-6
HG_HEAD_DIM = 128
SSM_HEAD_DIM = 64
SSM_GROUPS = 8
SSM_CONV_HALO = 2
FFN_CONV_HALO = 1
SUBLANES = 8
LANES = 128
CHUNK = 128
CHUNK_STREAM = CHUNK // SUBLANES
LOG2E = 1.4426950408889634
GLA_SAFE_LOG_DECAY = 75.0
GLA_STEP_CHUNKS = 2
VMEM_LIMIT_BYTES = 56 * 1024 * 1024


def _cparams(*sem):
    return pltpu.CompilerParams(dimension_semantics=sem, vmem_limit_bytes=VMEM_LIMIT_BYTES)


def _const_spec(shape, index=None):
    index = (0,) * len(shape) if index is None else index
    return pl.BlockSpec(shape, lambda *_: index, pipeline_mode=pl.Buffered(1))


def _layer_spec(w, layer, n_col_blocks=1, col_block=0):
    _, k, n = w.shape
    return pl.BlockSpec((None, k, n // n_col_blocks), lambda *_: (layer, 0, col_block),
                        pipeline_mode=pl.Buffered(1))


def _wide_layer_spec(w, layer, width, col_block):
    return pl.BlockSpec((None, w.shape[1], width), lambda *_: (layer, 0, col_block), pipeline_mode=pl.Buffered(1))


def _rmsnorm(x, w):
    return x * lax.rsqrt(jnp.mean(x * x, axis=-1, keepdims=True) + EPS) * w


def _sigmoid(x):
    return 0.5 * jnp.tanh(0.5 * x) + 0.5


def _silu_of_twice(h):
    return h * jnp.tanh(h) + h


def _silu(x):
    return _silu_of_twice(0.5 * x)


def _softplus(x):
    return jnp.maximum(x, 0.0) + jnp.log(1.0 + jnp.exp(-jnp.abs(x)))


def _dot(a, b):
    return jnp.dot(a, b, preferred_element_type=F32)


def _dot_nt(a, b):
    return lax.dot_general(a, b, (((1,), (1,)), ((), ())), preferred_element_type=F32)


def _dot_tn(a, b):
    return lax.dot_general(a, b, (((0,), (0,)), ((), ())), preferred_element_type=F32)


def _split3(x):
    hi = x.astype(BF16)
    r1 = x - hi.astype(F32)
    mid = r1.astype(BF16)
    lo = (r1 - mid.astype(F32)).astype(BF16)
    return hi, mid, lo


def _sel_rows(m01, x):
    hi, mid, lo = _split3(x)
    return _dot(m01, hi) + _dot(m01, mid) + _dot(m01, lo)


def _iota2(shape, dim):
    return lax.broadcasted_iota(jnp.int32, shape, dim)


def _chunk_token(idx):
    return (idx & (SUBLANES - 1)) * CHUNK_STREAM + (idx >> 3)


def _chunk_row(tok):
    return (tok % CHUNK_STREAM) * SUBLANES + tok // CHUNK_STREAM


def _chunk_running_sum(x, reverse):
    n = CHUNK // SUBLANES
    order = range(n - 1, -1, -1) if reverse else range(n)
    tiles, acc = [None] * n, None
    for i in order:
        t = x[i * SUBLANES:(i + 1) * SUBLANES]
        acc = t if acc is None else acc + t
        tiles[i] = acc
    sub = _iota2(acc.shape, 0)
    scan = acc
    for k in (1, 2, 4):
        if reverse:
            scan = scan + jnp.where(sub < SUBLANES - k, pltpu.roll(scan, SUBLANES - k, 0), 0.0)
        else:
            scan = scan + jnp.where(sub >= k, pltpu.roll(scan, k, 0), 0.0)
    offset = scan - acc
    return jnp.concatenate([t + offset for t in tiles], axis=0)


def _interleave_rows(x, slab_ref):
    tm, d = x.shape
    for s in range(d // LANES):
        for g in range(0, tm, CHUNK):
            for r in range(SUBLANES):
                slab_ref[s, pl.ds(g + r, CHUNK_STREAM, stride=SUBLANES), :] = (
                    x[g + r * CHUNK_STREAM: g + (r + 1) * CHUNK_STREAM, s * LANES:(s + 1) * LANES])
    return jnp.concatenate([slab_ref[s] for s in range(d // LANES)], axis=1)


def _deinterleave_add(res_ref, slab_ref, out_ref, y):
    tm, d = y.shape
    for s in range(d // LANES):
        slab_ref[s] = y[:, s * LANES:(s + 1) * LANES]
    for s in range(d // LANES):
        for g in range(0, tm, CHUNK):
            for r in range(SUBLANES):
                rows = slice(g + r * CHUNK_STREAM, g + (r + 1) * CHUNK_STREAM)
                lanes = slice(s * LANES, (s + 1) * LANES)
                out_ref[rows, lanes] = res_ref[rows, lanes] + slab_ref[s, pl.ds(g + r, CHUNK_STREAM, stride=SUBLANES), :]


def _hgrn_proj_kernel(h_ref, nw_ref, w_ref, lbl_ref, q_ref, lf_ref, kf_ref, lb_ref, kb_ref, v_ref, g_ref,
                      slab_ref, *, layer, col_chunk):
    d = h_ref.shape[1]
    u = _rmsnorm(_interleave_rows(h_ref[...], slab_ref), nw_ref[...]).astype(BF16)
    lg = lbl_ref[...]
    n_rows = lg.shape[0]
    mx = lg[0:1]
    for r in range(1, n_rows):
        mx = jnp.maximum(mx, lg[r:r + 1])
    es = [jnp.exp(lg[r:r + 1] - mx) for r in range(n_rows)]
    tot = es[0]
    for r in range(1, n_rows):
        tot = tot + es[r]
    part = es[0]
    for r in range(1, layer + 1):
        part = part + es[r]
    lb = part / tot

    def gate(fr, cols, l_ref, k_ref):
        lbc = lb[:, cols]
        f = lbc + (1.0 - lbc) * _sigmoid(fr)
        l_ref[:, cols] = jnp.log2(f)
        k_ref[:, cols] = (1.0 - f).astype(BF16)

    for sec in range(5):
        for c0 in range(0, d, col_chunk):
            cols = slice(c0, c0 + col_chunk)
            p = _dot(u, w_ref[:, sec * d + c0: sec * d + c0 + col_chunk])
            if sec == 0:
                q_ref[:, cols] = _silu(p).astype(BF16)
            elif sec == 1:
                gate(p, cols, lf_ref, kf_ref)
            elif sec == 2:
                gate(p, cols, lb_ref, kb_ref)
            elif sec == 3:
                v_ref[:, cols] = p.astype(BF16)
            else:
                g_ref[:, cols] = _silu(p).astype(BF16)


def _hgrn_proj(h2, norm_w, w_in_bf, j, lb_logits, layer, tm):
    t, d = h2.shape
    tile = pl.BlockSpec((tm, d), lambda i: (i, 0))
    tok = lambda dt: jax.ShapeDtypeStruct((t, d), dt)
    return pl.pallas_call(
        functools.partial(_hgrn_proj_kernel, layer=layer, col_chunk=min(512, d)),
        grid=(t // tm,),
        in_specs=[tile, _const_spec((1, d)), _layer_spec(w_in_bf, j), _const_spec(lb_logits.shape)],
        out_specs=[tile] * 7,
        out_shape=[tok(BF16), tok(F32), tok(BF16), tok(F32), tok(BF16), tok(BF16), tok(BF16)],
        scratch_shapes=[pltpu.VMEM((d // LANES, tm, LANES), F32)],
        compiler_params=_cparams("parallel"),
        name="hgrn_proj",
    )(h2, norm_w.reshape(1, d), w_in_bf, lb_logits)


def _gla_step(q_ref, k_ref, v_ref, o_ref, st_ref, *, rows, reverse, safe, b_all):
    c = CHUNK
    n_heads = st_ref.shape[0]
    hd = st_ref.shape[2]
    row = _chunk_token(_iota2((c, c), 0))
    col = _chunk_token(_iota2((c, c), 1))
    last = 0 if reverse else c - 1
    mid = _chunk_row(c // 2 if reverse else c // 2 - 1)
    for h in range(n_heads):
        hs = slice(h * hd, (h + 1) * hd)
        b = b_all[:, hs]
        bl = b[last:last + 1, :]
        vb = v_ref[0, rows, hs]
        st = st_ref[h]
        if safe:
            bm = b[mid:mid + 1, :]
            dm = b - bm
            q_hat = q_ref[0, rows, hs] * jnp.exp2(dm).astype(BF16)
            k_hat = k_ref[0, rows, hs] * jnp.exp2(-dm).astype(BF16)
            keep = (row <= col) if reverse else (row >= col)
            qk = _dot_nt(q_hat, jnp.concatenate([k_hat, (st * jnp.exp2(bm)).astype(BF16)], axis=0))
            att = jnp.where(keep, qk[:, :c], 0.0)
            o = qk[:, c:] + _dot(att.astype(BF16), vb)
            kv = _dot_tn(vb, k_hat) * jnp.exp2(bl - bm)
        else:
            qf = q_ref[0, rows, hs].astype(F32)
            kf = k_ref[0, rows, hs].astype(F32)
            att = jnp.where(row == col, _dot_nt(q_ref[0, rows, hs], k_ref[0, rows, hs]), 0.0)
            lvl = 0
            while (1 << lvl) < c:
                bs = 1 << lvl
                edge = (row >> (lvl + 1)) * (2 * bs) + (bs if reverse else bs - 1)
                g = _sel_rows((col == edge).astype(BF16), b)
                q_l = (qf * jnp.exp2(jnp.minimum(b - g, 0.0))).astype(BF16)
                k_l = (kf * jnp.exp2(jnp.minimum(g - b, 0.0))).astype(BF16)
                same_parent = (row >> (lvl + 1)) == (col >> (lvl + 1))
                if reverse:
                    pair = same_parent & ((col >> lvl) == (row >> lvl) + 1)
                else:
                    pair = same_parent & ((row >> lvl) == (col >> lvl) + 1)
                att = att + jnp.where(pair, _dot_nt(q_l, k_l), 0.0)
                lvl += 1
            q_in = (qf * jnp.exp2(b)).astype(BF16)
            o = _dot_nt(q_in, st.astype(BF16)) + _dot(att.astype(BF16), vb)
            kv = _dot_tn(vb, (kf * jnp.exp2(bl - b)).astype(BF16))
        o_ref[0, rows, hs] = o.astype(o_ref.dtype)
        st_ref[h] = st * jnp.exp2(bl) + kv


def _gla_kernel(qf_ref, lf_ref, kf_ref, vf_ref, qb_ref, lb_ref, kb_ref, vb_ref, of_ref, ob_ref, sf_ref, sb_ref):
    @pl.when(pl.program_id(1) == 0)
    def _():
        sf_ref[...] = jnp.zeros_like(sf_ref)
        sb_ref[...] = jnp.zeros_like(sb_ref)

    def half_chunk_decay(l_ref, rows):
        tot = l_ref[0, rows.start:rows.start + SUBLANES, :]
        for i in range(1, CHUNK // SUBLANES):
            tot = tot + l_ref[0, rows.start + i * SUBLANES:rows.start + (i + 1) * SUBLANES, :]
        s = tot + pltpu.roll(tot, 1, 0)
        s = s + pltpu.roll(s, 2, 0)
        lo, hi = SUBLANES // 2 - 1, SUBLANES - 1
        return jnp.minimum(s[lo:lo + 1, :], s[hi:hi + 1, :])

    n_sub = qf_ref.shape[1] // CHUNK
    for i in range(n_sub):
        rf = slice(i * CHUNK, (i + 1) * CHUNK)
        rb = slice((n_sub - 1 - i) * CHUNK, (n_sub - i) * CHUNK)
        worst = jnp.min(jnp.minimum(half_chunk_decay(lf_ref, rf), half_chunk_decay(lb_ref, rb)))
        safe = worst >= -GLA_SAFE_LOG_DECAY * LOG2E
        b_fw = _chunk_running_sum(lf_ref[0, rf, :], reverse=False)
        b_bw = _chunk_running_sum(lb_ref[0, rb, :], reverse=True)

        for flag, pred in ((True, safe), (False, jnp.logical_not(safe))):
            @pl.when(pred)
            def _(flag=flag, rf=rf, rb=rb, b_fw=b_fw, b_bw=b_bw):
                _gla_step(qf_ref, kf_ref, vf_ref, of_ref, sf_ref, rows=rf, reverse=False, safe=flag, b_all=b_fw)
                _gla_step(qb_ref, kb_ref, vb_ref, ob_ref, sb_ref, rows=rb, reverse=True, safe=flag, b_all=b_bw)


def _gla(q, l_fw, k_fw, l_bw, k_bw, v, chunk):
    bsz, seq, d = q.shape
    nc = seq // chunk
    n_heads = d // HG_HEAD_DIM
    fw = pl.BlockSpec((1, chunk, d), lambda b, c: (b, c, 0))
    bw = pl.BlockSpec((1, chunk, d), lambda b, c: (b, nc - 1 - c, 0))
    state = pltpu.VMEM((n_heads, HG_HEAD_DIM, HG_HEAD_DIM), F32)
    return pl.pallas_call(
        _gla_kernel,
        grid=(bsz, nc),
        in_specs=[fw] * 4 + [bw] * 4,
        out_specs=[fw, bw],
        out_shape=[jax.ShapeDtypeStruct((bsz, seq, d), BF16)] * 2,
        scratch_shapes=[state, state],
        compiler_params=_cparams("parallel", "arbitrary"),
        name="hgrn_gla",
    )(q, l_fw, k_fw, v, q, l_bw, k_bw, v)


def _hgrn_out_kernel(of_ref, ob_ref, g_ref, h_ref, nw_ref, w_ref, out_ref, slab_ref):
    d = h_ref.shape[1]
    nw = nw_ref[...]
    parts = []
    for c0 in range(0, d, HG_HEAD_DIM):
        hs = slice(c0, c0 + HG_HEAD_DIM)
        o = of_ref[:, hs].astype(F32) + ob_ref[:, hs].astype(F32)
        y = _rmsnorm(o, nw) * g_ref[:, hs].astype(F32)
        parts.append(y.astype(BF16))
    y = jnp.concatenate(parts, axis=1)
    _deinterleave_add(h_ref, slab_ref, out_ref, _dot(y, w_ref[...]))


def _hgrn_out(o_fw, o_bw, g, h2, norm_w, w_out_bf, j, tm):
    t, d = h2.shape
    tile = pl.BlockSpec((tm, d), lambda i: (i, 0))
    return pl.pallas_call(
        _hgrn_out_kernel,
        grid=(t // tm,),
        in_specs=[tile, tile, tile, tile, _const_spec((1, HG_HEAD_DIM)), _layer_spec(w_out_bf, j)],
        out_specs=tile,
        out_shape=jax.ShapeDtypeStruct((t, d), F32),
        scratch_shapes=[pltpu.VMEM((d // LANES, tm, LANES), F32)],
        compiler_params=_cparams("parallel"),
        name="hgrn_out",
    )(o_fw, o_bw, g, h2, norm_w.reshape(1, HG_HEAD_DIM), w_out_bf)


def _halo_rows(hp_ref, hn_ref, nw):
    up = _rmsnorm(hp_ref[0], nw)
    un = _rmsnorm(hn_ref[0], nw)
    return jnp.concatenate([up, un], axis=0).astype(BF16)


def _halo_inside():
    i = pl.program_id(1)
    r = _iota2((2 * SUBLANES, 1), 0)
    has_prev = (i > 0).astype(F32)
    has_next = (i < pl.num_programs(1) - 1).astype(F32)
    return jnp.where(r < SUBLANES, has_prev, has_next) > 0.5


def _shifted(e_ref, x, xh, halo, n_rows):
    e_ref[0:SUBLANES, :] = xh[0:SUBLANES]
    e_ref[SUBLANES:SUBLANES + n_rows, :] = x
    e_ref[SUBLANES + n_rows:, :] = xh[SUBLANES:]
    return [x if j == halo else e_ref[pl.ds(SUBLANES - halo + j, n_rows), :] for j in range(2 * halo + 1)]


def _ffn_kernel(h_ref, hp_ref, hn_ref, nw_ref, wg_ref, wv_ref, cw_ref, cb_ref, wo_ref, fw_ref,
                out_ref, a_ref, e_ref, *, col_chunk, final_norm):
    tm = h_ref.shape[1]
    d_ff = wg_ref.shape[1]
    nw = nw_ref[...]
    hx = h_ref[0]
    u = _rmsnorm(hx, nw).astype(BF16)
    uh = _halo_rows(hp_ref, hn_ref, nw)
    inside = _halo_inside()
    for c0 in range(0, d_ff, col_chunk):
        cols = slice(c0, c0 + col_chunk)
        gate = _dot(u, wg_ref[:, cols])
        gh = jnp.where(inside, _dot(uh, wg_ref[:, cols]), 0.0)
        taps = _shifted(e_ref, gate, gh, FFN_CONV_HALO, tm)
        half_conv = 0.5 * cb_ref[:, cols]
        for j, tap in enumerate(taps):
            half_conv = half_conv + (0.5 * cw_ref[j:j + 1, cols]) * tap
        val = _dot(u, wv_ref[:, cols])
        a_ref[:, cols] = (_silu_of_twice(half_conv) * val).astype(BF16)
    y = hx + _dot(a_ref[...], wo_ref[...])
    if final_norm:
        y = _rmsnorm(y, fw_ref[...])
    out_ref[0] = y


def _halo_specs(tm, d, seq):
    nb = tm // SUBLANES
    n_blocks = seq // SUBLANES
    prev = pl.BlockSpec((1, SUBLANES, d), lambda b, i: (b, jnp.maximum(i * nb - 1, 0), 0))
    nxt = pl.BlockSpec((1, SUBLANES, d), lambda b, i: (b, jnp.minimum((i + 1) * nb, n_blocks - 1), 0))
    return prev, nxt


def _ffn(h3, norm_w, w_in_bf, conv_w, conv_b, wo_bf, layer, final_w, tm, final_norm):
    bsz, seq, d = h3.shape
    d_ff = wo_bf.shape[1]
    col_chunk = 256
    tile = pl.BlockSpec((1, tm, d), lambda b, i: (b, i, 0))
    prev, nxt = _halo_specs(tm, d, seq)
    return pl.pallas_call(
        functools.partial(_ffn_kernel, col_chunk=col_chunk, final_norm=final_norm),
        grid=(bsz, seq // tm),
        in_specs=[tile, prev, nxt, _const_spec((1, d)), _layer_spec(w_in_bf, layer, 2, 0),
                  _layer_spec(w_in_bf, layer, 2, 1), _const_spec(conv_w.shape), _const_spec((1, d_ff)),
                  _layer_spec(wo_bf, layer), _const_spec((1, d))],
        out_specs=tile,
        out_shape=jax.ShapeDtypeStruct(h3.shape, F32),
        scratch_shapes=[pltpu.VMEM((tm, d_ff), BF16), pltpu.VMEM((tm + 2 * SUBLANES, col_chunk), F32)],
        compiler_params=_cparams("parallel", "parallel"),
        name="convglu_final" if final_norm else "convglu",
    )(h3, h3, h3, norm_w.reshape(1, d), w_in_bf, w_in_bf, conv_w, conv_b.reshape(1, d_ff), wo_bf,
      final_w.reshape(1, d))


def _interleaved_taps(x, xh, halo):
    tm, w = x.shape
    nv = CHUNK // SUBLANES
    n_chunks = tm // CHUNK
    sub = _iota2((SUBLANES, w), 0)
    tile = lambda g, j: x[g * CHUNK + j * SUBLANES: g * CHUNK + (j + 1) * SUBLANES]

    def after(g, j):
        if g + 1 < n_chunks:
            return tile(g + 1, j)
        nxt = xh[SUBLANES:]
        return nxt if j == 0 else pltpu.roll(nxt, SUBLANES - j, 0)

    def before(g, j):
        if g > 0:
            return tile(g - 1, j)
        prv = xh[:SUBLANES]
        return prv if j == nv - 1 else pltpu.roll(prv, nv - 1 - j, 0)

    taps = [[] for _ in range(2 * halo + 1)]
    for g in range(n_chunks):
        lead = [pltpu.roll(jnp.where(sub == SUBLANES - 1, before(g, nv - halo + m), tile(g, nv - halo + m)), 1, 0)
                for m in range(halo)]
        tail = [pltpu.roll(jnp.where(sub == 0, after(g, m), tile(g, m)), SUBLANES - 1, 0) for m in range(halo)]
        ext = jnp.concatenate(lead + [x[g * CHUNK:(g + 1) * CHUNK]] + tail, axis=0)
        for j in range(2 * halo + 1):
            taps[j].append(ext[j * SUBLANES: j * SUBLANES + CHUNK])
    return [jnp.concatenate(t, axis=0) for t in taps]


def _ssm_proj_kernel(h_ref, hp_ref, hn_ref, nw_ref, wz_ref, wx_ref, wbc_ref, wdt_ref, cw_ref, cb_ref, dtb_ref,
                     z_ref, xs_ref, bm_ref, cm_ref, dt_ref, slab_ref, *, col_chunk):
    d_inner = z_ref.shape[2]
    gn = bm_ref.shape[2]
    nw = nw_ref[...]
    u = _rmsnorm(_interleave_rows(h_ref[0], slab_ref), nw).astype(BF16)
    uh = _halo_rows(hp_ref, hn_ref, nw)
    inside = _halo_inside()
    for c0 in range(0, d_inner, col_chunk):
        cols = slice(c0, c0 + col_chunk)
        z_ref[0, :, cols] = _silu(_dot(u, wz_ref[:, cols])).astype(BF16)
    for c0 in range(0, d_inner + 2 * gn, col_chunk):
        cols = slice(c0, c0 + col_chunk)
        w = wx_ref[:, cols] if c0 < d_inner else wbc_ref[:, c0 - d_inner: c0 - d_inner + col_chunk]
        x = _dot(u, w)
        xh = jnp.where(inside, _dot(uh, w), 0.0)
        taps = _interleaved_taps(x, xh, SSM_CONV_HALO)
        half_conv = 0.5 * cb_ref[:, cols]
        for j, tap in enumerate(taps):
            half_conv = half_conv + (0.5 * cw_ref[j:j + 1, cols]) * tap
        act = _silu_of_twice(half_conv).astype(BF16)
        if c0 < d_inner:
            xs_ref[0, :, cols] = act
        elif c0 < d_inner + gn:
            bm_ref[0, :, c0 - d_inner: c0 - d_inner + col_chunk] = act
        else:
            cm_ref[0, :, c0 - d_inner - gn: c0 - d_inner - gn + col_chunk] = act
    dt_ref[0] = _softplus(_dot(u, wdt_ref[...]) + dtb_ref[...])


def _ssm_proj(h3, norm_w, w_in_bf, j, wdt_bf, conv_w, conv_b, dt_bias_row, d_inner, tm):
    bsz, seq, d = h3.shape
    conv_dim = conv_w.shape[1]
    gn = (conv_dim - d_inner) // 2
    assert 2 * gn == d_inner
    col_chunk = 512
    tile = pl.BlockSpec((1, tm, d), lambda b, i: (b, i, 0))
    prev, nxt = _halo_specs(tm, d, seq)
    out_tile = lambda w: pl.BlockSpec((1, tm, w), lambda b, i: (b, i, 0))
    tok = lambda w, dt: jax.ShapeDtypeStruct((bsz, seq, w), dt)
    return pl.pallas_call(
        functools.partial(_ssm_proj_kernel, col_chunk=col_chunk),
        grid=(bsz, seq // tm),
        in_specs=[tile, prev, nxt, _const_spec((1, d)), _wide_layer_spec(w_in_bf, j, d_inner, 0),
                  _wide_layer_spec(w_in_bf, j, d_inner, 1), _wide_layer_spec(w_in_bf, j, d_inner, 2),
                  _const_spec(wdt_bf.shape), _const_spec(conv_w.shape), _const_spec((1, conv_dim)),
                  _const_spec((1, LANES))],
        out_specs=[out_tile(d_inner), out_tile(d_inner), out_tile(gn), out_tile(gn), out_tile(LANES)],
        out_shape=[tok(d_inner, BF16), tok(d_inner, BF16), tok(gn, BF16), tok(gn, BF16), tok(LANES, F32)],
        scratch_shapes=[pltpu.VMEM((d // LANES, tm, LANES), F32)],
        compiler_params=_cparams("parallel", "parallel"),
        name="ssm_proj",
    )(h3, h3, h3, norm_w.reshape(1, d), w_in_bf, w_in_bf, w_in_bf, wdt_bf, conv_w, conv_b.reshape(1, conv_dim),
      dt_bias_row)


def _ssd_step(xs_ref, bm_ref, cm_ref, dt_ref, z_ref, alog_ref, y_ref, st_ref, *, reverse, skip_ref=None):
    c = xs_ref.shape[1]
    n_groups = st_ref.shape[0]
    n_state = st_ref.shape[1]
    gw = st_ref.shape[2]
    heads = n_groups * (gw // SSM_HEAD_DIM)
    h0 = heads if reverse else 0
    row = _chunk_token(_iota2((c, c), 0))
    col = _chunk_token(_iota2((c, c), 1))
    lane = _iota2((c, LANES), 1)
    keep = (row <= col) if reverse else (row >= col)
    last = 0 if reverse else c - 1

    dt = dt_ref[0]
    la = dt * (-LOG2E * jnp.exp(alog_ref[...]))
    a_col = _chunk_running_sum(la, reverse)
    a_row = a_col.T
    a_src = a_row - jnp.log(dt.T) * LOG2E
    w_row_bf = jnp.exp2(a_row[:, last:last + 1] - a_src).astype(BF16)

    hpg = gw // SSM_HEAD_DIM
    lane_g = _iota2((c, gw), 1)
    for g in range(n_groups):
        bg = bm_ref[0, :, g * n_state:(g + 1) * n_state]
        cg = cm_ref[0, :, g * n_state:(g + 1) * n_state]
        bt = bg.astype(F32).T.astype(BF16)
        st = st_ref[g]
        cg_out = _dot(cg, jnp.concatenate([bt, st.astype(BF16)], axis=1))
        cb = cg_out[:, :c].astype(BF16)
        y_in = cg_out[:, c:]
        xg = xs_ref[0, :, g * gw:(g + 1) * gw]
        zero = jnp.zeros_like(xg)
        ws, bws, bcs, x_blocks = [], [], [], []
        for j in range(hpg):
            hh = h0 + g * hpg + j
            bc = jnp.broadcast_to(a_col[:, hh:hh + 1], (c, c))
            decay = jnp.exp2(jnp.where(keep, bc - a_src[hh:hh + 1, :], -jnp.inf))
            ws.append(cb * decay.astype(BF16))
            bws.append(bt * w_row_bf[hh:hh + 1, :])
            bcs.append(bc)
            x_blocks.append(jnp.where((lane_g >= j * SSM_HEAD_DIM) & (lane_g < (j + 1) * SSM_HEAD_DIM), xg, zero))
        lhs = jnp.concatenate([jnp.concatenate(ws, axis=1), jnp.concatenate(bws, axis=1)], axis=0)
        res = _dot(lhs, jnp.concatenate(x_blocks, axis=0))
        e_parts = []
        for p in range(gw // LANES):
            e_parts.append(jnp.exp2(jnp.where(lane < SSM_HEAD_DIM, bcs[2 * p], bcs[2 * p + 1])))
        e_t = jnp.concatenate(e_parts, axis=1)
        y = res[:c] + y_in * e_t
        if skip_ref is not None:
            y = y + xg.astype(F32) * skip_ref[:, g * gw:(g + 1) * gw]
        y_ref[0, :, g * gw:(g + 1) * gw] = (y * z_ref[0, :, g * gw:(g + 1) * gw].astype(F32)).astype(y_ref.dtype)
        st_ref[g] = st * e_t[last:last + 1, :] + res[c:]


def _ssd_kernel(xf_ref, bf_ref, cf_ref, dtf_ref, zf_ref, xb_ref, bb_ref, cb_ref, dtb_ref, zb_ref, alog_ref, dsk_ref,
                yf_ref, yb_ref, sf_ref, sb_ref):
    @pl.when(pl.program_id(1) == 0)
    def _():
        sf_ref[...] = jnp.zeros_like(sf_ref)
        sb_ref[...] = jnp.zeros_like(sb_ref)

    _ssd_step(xf_ref, bf_ref, cf_ref, dtf_ref, zf_ref, alog_ref, yf_ref, sf_ref, reverse=False, skip_ref=dsk_ref)
    _ssd_step(xb_ref, bb_ref, cb_ref, dtb_ref, zb_ref, alog_ref, yb_ref, sb_ref, reverse=True)


def _ssd(xs, bm, cm, dt, z, alog_row, d_skip_row, chunk):
    bsz, seq, d_inner = xs.shape
    gn = bm.shape[2]
    nc = seq // chunk
    n_state = gn // SSM_GROUPS
    fw = lambda w: pl.BlockSpec((1, chunk, w), lambda b, c: (b, c, 0))
    bw = lambda w: pl.BlockSpec((1, chunk, w), lambda b, c: (b, nc - 1 - c, 0))
    state = pltpu.VMEM((SSM_GROUPS, n_state, d_inner // SSM_GROUPS), F32)
    return pl.pallas_call(
        _ssd_kernel,
        grid=(bsz, nc),
        in_specs=[fw(d_inner), fw(gn), fw(gn), fw(LANES), fw(d_inner), bw(d_inner), bw(gn), bw(gn), bw(LANES),
                  bw(d_inner), _const_spec((1, LANES)), _const_spec((1, d_inner))],
        out_specs=[fw(d_inner), bw(d_inner)],
        out_shape=[jax.ShapeDtypeStruct(xs.shape, BF16)] * 2,
        scratch_shapes=[state, state],
        compiler_params=_cparams("parallel", "arbitrary"),
        name="ssm_ssd",
    )(xs, bm, cm, dt, z, xs, bm, cm, dt, z, alog_row, d_skip_row)


def _ssm_out_kernel(yf_ref, yb_ref, h_ref, nw_ref, w_ref, out_ref, slab_ref, *, group_w):
    d_inner = yf_ref.shape[1]
    parts = []
    for c0 in range(0, d_inner, group_w):
        cs = slice(c0, c0 + group_w)
        y = yf_ref[:, cs].astype(F32) + yb_ref[:, cs].astype(F32)
        parts.append(_rmsnorm(y, nw_ref[:, cs]).astype(BF16))
    y = jnp.concatenate(parts, axis=1)
    _deinterleave_add(h_ref, slab_ref, out_ref, _dot(y, w_ref[...]))


def _ssm_out(y_fw, y_bw, h2, norm_w, w_out_bf, j, tm):
    t, d = h2.shape
    d_inner = y_fw.shape[1]
    wide = pl.BlockSpec((tm, d_inner), lambda i: (i, 0))
    tile = pl.BlockSpec((tm, d), lambda i: (i, 0))
    return pl.pallas_call(
        functools.partial(_ssm_out_kernel, group_w=d_inner // SSM_GROUPS),
        grid=(t // tm,),
        in_specs=[wide, wide, tile, _const_spec((1, d_inner)), _layer_spec(w_out_bf, j)],
        out_specs=tile,
        out_shape=jax.ShapeDtypeStruct((t, d), F32),
        scratch_shapes=[pltpu.VMEM((d // LANES, tm, LANES), F32)],
        compiler_params=_cparams("parallel"),
        name="ssm_out",
    )(y_fw, y_bw, h2, norm_w.reshape(1, d_inner), w_out_bf)


def kernel(x, norm1_w, norm2_w, a_w_in, a_lb_logits, a_norm_w, a_w_out, b_w_in, b_conv_w, b_conv_b,
           b_dt_bias, b_a_log, b_d_skip, b_norm_w, b_w_out, ffn_w_in, ffn_conv_w, ffn_conv_b, ffn_w_out,
           final_norm_w):
    bsz, seq, d = x.shape
    depth = norm1_w.shape[0]
    t = bsz * seq
    tm = min(512, seq)
    d_inner = b_norm_w.shape[1]
    n_ssm_heads = b_dt_bias.shape[2]
    assert seq % tm == 0 and tm % CHUNK == 0 and seq % (GLA_STEP_CHUNKS * CHUNK) == 0
    assert 2 * n_ssm_heads <= LANES and d_inner == n_ssm_heads * SSM_HEAD_DIM

    bf = lambda w: w.astype(BF16)
    a_w_in, a_w_out, b_w_out, ffn_w_in, ffn_w_out = map(bf, (a_w_in, a_w_out, b_w_out, ffn_w_in, ffn_w_out))
    n_zxbc = d_inner + b_conv_w.shape[2]
    b_w_dt = bf(b_w_in[:, :, n_zxbc:])
    b_w_in = bf(lax.optimization_barrier(b_w_in[:, :, :n_zxbc]))
    h = x
    for i in range(depth):
        j = i // 2
        if i % 2 == 0:
            q, l_fw, k_fw, l_bw, k_bw, v, g = _hgrn_proj(h.reshape(t, d), norm1_w[i], a_w_in, j, a_lb_logits, i, tm)
            r3 = lambda a: a.reshape(bsz, seq, d)
            o_fw, o_bw = _gla(r3(q), r3(l_fw), r3(k_fw), r3(l_bw), r3(k_bw), r3(v), GLA_STEP_CHUNKS * CHUNK)
            h = _hgrn_out(o_fw.reshape(t, d), o_bw.reshape(t, d), g, h.reshape(t, d), a_norm_w[j], a_w_out, j,
                          min(2 * tm, seq)).reshape(bsz, seq, d)
        else:
            pad = LANES - 2 * n_ssm_heads
            wdt = jnp.pad(b_w_dt[j], ((0, 0), (0, pad)))
            dt_bias_row = jnp.pad(b_dt_bias[j].reshape(1, -1), ((0, 0), (0, pad)))
            alog_row = jnp.pad(b_a_log[j].reshape(1, -1), ((0, 0), (0, pad)))
            z, xs, bm, cm, dt = _ssm_proj(h, norm1_w[i], b_w_in, j, wdt, b_conv_w[j], b_conv_b[j], dt_bias_row,
                                          d_inner, tm)
            d_skip_row = jnp.repeat(b_d_skip[j], SSM_HEAD_DIM).reshape(1, d_inner)
            y_fw, y_bw = _ssd(xs, bm, cm, dt, z, alog_row, d_skip_row, CHUNK)
            f2 = lambda a: a.reshape(t, d_inner)
            h = _ssm_out(f2(y_fw), f2(y_bw), h.reshape(t, d), b_norm_w[j], b_w_out, j,
                         min(2 * tm, seq)).reshape(bsz, seq, d)
        h = _ffn(h, norm2_w[i], ffn_w_in, ffn_conv_w[i], ffn_conv_b[i], ffn_w_out, i, final_norm_w,
                 min(2 * tm, seq), final_norm=(i == depth - 1))
    return h
```

```python
import functools

import jax
import jax.numpy as jnp
from jax import lax
from jax.experimental import pallas as pl
from jax.experimental.pallas import tpu as pltpu

F32 = jnp.float32
BF16 = jnp.bfloat16

EPS = 1e-6
HG_HEAD_DIM = 128
SSM_HEAD_DIM = 64
SSM_GROUPS = 8
SSM_CONV_HALO = 2
FFN_CONV_HALO = 1
SUBLANES = 8
LANES = 128
CHUNK = 128
CHUNK_STREAM = CHUNK // SUBLANES
LOG2E = 1.4426950408889634
GLA_SAFE_LOG_DECAY = 75.0
REC_STEP_CHUNKS = 2
VMEM_LIMIT_BYTES = 56 * 1024 * 1024


def _cparams(*sem):
    return pltpu.CompilerParams(dimension_semantics=sem, vmem_limit_bytes=VMEM_LIMIT_BYTES)


def _const_spec(shape, index=None):
    index = (0,) * len(shape) if index is None else index
    return pl.BlockSpec(shape, lambda *_: index, pipeline_mode=pl.Buffered(1))


def _layer_spec(w, layer, n_col_blocks=1, col_block=0):
    _, k, n = w.shape
    return pl.BlockSpec((None, k, n // n_col_blocks), lambda *_: (layer, 0, col_block),
                        pipeline_mode=pl.Buffered(1))


def _wide_layer_spec(w, layer, width, col_block):
    return pl.BlockSpec((None, w.shape[1], width), lambda *_: (layer, 0, col_block), pipeline_mode=pl.Buffered(1))


def _rmsnorm(x, w):
    return x * lax.rsqrt(jnp.mean(x * x, axis=-1, keepdims=True) + EPS) * w


def _sigmoid(x):
    return 0.5 * jnp.tanh(0.5 * x) + 0.5


def _silu_of_twice(h):
    return h * jnp.tanh(h) + h


def _silu(x):
    return _silu_of_twice(0.5 * x)


def _softplus(x):
    return jnp.maximum(x, 0.0) + jnp.log(1.0 + jnp.exp(-jnp.abs(x)))


def _dot(a, b):
    return jnp.dot(a, b, preferred_element_type=F32)


def _dot_nt(a, b):
    return lax.dot_general(a, b, (((1,), (1,)), ((), ())), preferred_element_type=F32)


def _dot_tn(a, b):
    return lax.dot_general(a, b, (((0,), (0,)), ((), ())), preferred_element_type=F32)


def _split3(x):
    hi = x.astype(BF16)
    r1 = x - hi.astype(F32)
    mid = r1.astype(BF16)
    lo = (r1 - mid.astype(F32)).astype(BF16)
    return hi, mid, lo


def _sel_rows(m01, x):
    hi, mid, lo = _split3(x)
    return _dot(m01, hi) + _dot(m01, mid) + _dot(m01, lo)


def _iota2(shape, dim):
    return lax.broadcasted_iota(jnp.int32, shape, dim)


def _chunk_token(idx):
    return (idx & (SUBLANES - 1)) * CHUNK_STREAM + (idx >> 3)


def _chunk_row(tok):
    return (tok % CHUNK_STREAM) * SUBLANES + tok // CHUNK_STREAM


def _chunk_running_sum(x, reverse):
    n = CHUNK // SUBLANES
    order = range(n - 1, -1, -1) if reverse else range(n)
    tiles, acc = [None] * n, None
    for i in order:
        t = x[i * SUBLANES:(i + 1) * SUBLANES]
        acc = t if acc is None else acc + t
        tiles[i] = acc
    sub = _iota2(acc.shape, 0)
    scan = acc
    for k in (1, 2, 4):
        if reverse:
            scan = scan + jnp.where(sub < SUBLANES - k, pltpu.roll(scan, SUBLANES - k, 0), 0.0)
        else:
            scan = scan + jnp.where(sub >= k, pltpu.roll(scan, k, 0), 0.0)
    offset = scan - acc
    return jnp.concatenate([t + offset for t in tiles], axis=0)


def _interleave_rows(x, slab_ref):
    tm, d = x.shape
    for s in range(d // LANES):
        for g in range(0, tm, CHUNK):
            for r in range(SUBLANES):
                slab_ref[s, pl.ds(g + r, CHUNK_STREAM, stride=SUBLANES), :] = (
                    x[g + r * CHUNK_STREAM: g + (r + 1) * CHUNK_STREAM, s * LANES:(s + 1) * LANES])
    return jnp.concatenate([slab_ref[s] for s in range(d // LANES)], axis=1)


def _deinterleave_add(res_ref, slab_ref, out_ref, y):
    tm, d = y.shape
    for s in range(d // LANES):
        slab_ref[s] = y[:, s * LANES:(s + 1) * LANES]
    for s in range(d // LANES):
        for g in range(0, tm, CHUNK):
            for r in range(SUBLANES):
                rows = slice(g + r * CHUNK_STREAM, g + (r + 1) * CHUNK_STREAM)
                lanes = slice(s * LANES, (s + 1) * LANES)
                out_ref[rows, lanes] = res_ref[rows, lanes] + slab_ref[s, pl.ds(g + r, CHUNK_STREAM, stride=SUBLANES), :]


def _hgrn_proj_kernel(h_ref, nw_ref, w_ref, lbl_ref, q_ref, lf_ref, kf_ref, lb_ref, kb_ref, v_ref, g_ref,
                      slab_ref, *, layer, col_chunk):
    d = h_ref.shape[1]
    u = _rmsnorm(_interleave_rows(h_ref[...], slab_ref), nw_ref[...]).astype(BF16)
    lg = lbl_ref[...]
    n_rows = lg.shape[0]
    mx = lg[0:1]
    for r in range(1, n_rows):
        mx = jnp.maximum(mx, lg[r:r + 1])
    es = [jnp.exp(lg[r:r + 1] - mx) for r in range(n_rows)]
    tot = es[0]
    for r in range(1, n_rows):
        tot = tot + es[r]
    part = es[0]
    for r in range(1, layer + 1):
        part = part + es[r]
    lb = part / tot

    def gate(fr, cols, l_ref, k_ref):
        lbc = lb[:, cols]
        f = lbc + (1.0 - lbc) * _sigmoid(fr)
        l_ref[:, cols] = jnp.log2(f)
        k_ref[:, cols] = (1.0 - f).astype(BF16)

    for sec in range(5):
        for c0 in range(0, d, col_chunk):
            cols = slice(c0, c0 + col_chunk)
            p = _dot(u, w_ref[:, sec * d + c0: sec * d + c0 + col_chunk])
            if sec == 0:
                q_ref[:, cols] = _silu(p).astype(BF16)
            elif sec == 1:
                gate(p, cols, lf_ref, kf_ref)
            elif sec == 2:
                gate(p, cols, lb_ref, kb_ref)
            elif sec == 3:
                v_ref[:, cols] = p.astype(BF16)
            else:
                g_ref[:, cols] = _silu(p).astype(BF16)


def _hgrn_proj(h2, norm_w, w_in_bf, j, lb_logits, layer, tm):
    t, d = h2.shape
    tile = pl.BlockSpec((tm, d), lambda i: (i, 0))
    tok = lambda dt: jax.ShapeDtypeStruct((t, d), dt)
    return pl.pallas_call(
        functools.partial(_hgrn_proj_kernel, layer=layer, col_chunk=min(512, d)),
        grid=(t // tm,),
        in_specs=[tile, _const_spec((1, d)), _layer_spec(w_in_bf, j), _const_spec(lb_logits.shape)],
        out_specs=[tile] * 7,
        out_shape=[tok(BF16), tok(F32), tok(BF16), tok(F32), tok(BF16), tok(BF16), tok(BF16)],
        scratch_shapes=[pltpu.VMEM((d // LANES, tm, LANES), F32)],
        compiler_params=_cparams("parallel"),
        name="hgrn_proj",
    )(h2, norm_w.reshape(1, d), w_in_bf, lb_logits)


def _gla_step(q_ref, k_ref, v_ref, o_ref, st_ref, *, rows, reverse, safe, b_all):
    c = CHUNK
    n_heads = st_ref.shape[0]
    hd = st_ref.shape[2]
    row = _chunk_token(_iota2((c, c), 0))
    col = _chunk_token(_iota2((c, c), 1))
    last = 0 if reverse else c - 1
    mid = _chunk_row(c // 2 if reverse else c // 2 - 1)
    for h in range(n_heads):
        hs = slice(h * hd, (h + 1) * hd)
        b = b_all[:, hs]
        bl = b[last:last + 1, :]
        vb = v_ref[0, rows, hs]
        st = st_ref[h]
        if safe:
            bm = b[mid:mid + 1, :]
            dm = b - bm
            q_hat = q_ref[0, rows, hs] * jnp.exp2(dm).astype(BF16)
            k_hat = k_ref[0, rows, hs] * jnp.exp2(-dm).astype(BF16)
            keep = (row <= col) if reverse else (row >= col)
            qk = _dot_nt(q_hat, jnp.concatenate([k_hat, (st * jnp.exp2(bm)).astype(BF16)], axis=0))
            att = jnp.where(keep, qk[:, :c], 0.0)
            o = qk[:, c:] + _dot(att.astype(BF16), vb)
            kv = _dot_tn(vb, k_hat) * jnp.exp2(bl - bm)
        else:
            qf = q_ref[0, rows, hs].astype(F32)
            kf = k_ref[0, rows, hs].astype(F32)
            att = jnp.where(row == col, _dot_nt(q_ref[0, rows, hs], k_ref[0, rows, hs]), 0.0)
            lvl = 0
            while (1 << lvl) < c:
                bs = 1 << lvl
                edge = (row >> (lvl + 1)) * (2 * bs) + (bs if reverse else bs - 1)
                g = _sel_rows((col == edge).astype(BF16), b)
                q_l = (qf * jnp.exp2(jnp.minimum(b - g, 0.0))).astype(BF16)
                k_l = (kf * jnp.exp2(jnp.minimum(g - b, 0.0))).astype(BF16)
                same_parent = (row >> (lvl + 1)) == (col >> (lvl + 1))
                if reverse:
                    pair = same_parent & ((col >> lvl) == (row >> lvl) + 1)
                else:
                    pair = same_parent & ((row >> lvl) == (col >> lvl) + 1)
                att = att + jnp.where(pair, _dot_nt(q_l, k_l), 0.0)
                lvl += 1
            q_in = (qf * jnp.exp2(b)).astype(BF16)
            o = _dot_nt(q_in, st.astype(BF16)) + _dot(att.astype(BF16), vb)
            kv = _dot_tn(vb, (kf * jnp.exp2(bl - b)).astype(BF16))
        o_ref[0, rows, hs] = o.astype(o_ref.dtype)
        st_ref[h] = st * jnp.exp2(bl) + kv


def _gla_kernel(qf_ref, lf_ref, kf_ref, vf_ref, qb_ref, lb_ref, kb_ref, vb_ref, of_ref, ob_ref, sf_ref, sb_ref):
    @pl.when(pl.program_id(1) == 0)
    def _():
        sf_ref[...] = jnp.zeros_like(sf_ref)
        sb_ref[...] = jnp.zeros_like(sb_ref)

    def half_chunk_decay(l_ref, rows):
        tot = l_ref[0, rows.start:rows.start + SUBLANES, :]
        for i in range(1, CHUNK // SUBLANES):
            tot = tot + l_ref[0, rows.start + i * SUBLANES:rows.start + (i + 1) * SUBLANES, :]
        s = tot + pltpu.roll(tot, 1, 0)
        s = s + pltpu.roll(s, 2, 0)
        lo, hi = SUBLANES // 2 - 1, SUBLANES - 1
        return jnp.minimum(s[lo:lo + 1, :], s[hi:hi + 1, :])

    n_sub = qf_ref.shape[1] // CHUNK
    for i in range(n_sub):
        rf = slice(i * CHUNK, (i + 1) * CHUNK)
        rb = slice((n_sub - 1 - i) * CHUNK, (n_sub - i) * CHUNK)
        worst = jnp.min(jnp.minimum(half_chunk_decay(lf_ref, rf), half_chunk_decay(lb_ref, rb)))
        safe = worst >= -GLA_SAFE_LOG_DECAY * LOG2E
        b_fw = _chunk_running_sum(lf_ref[0, rf, :], reverse=False)
        b_bw = _chunk_running_sum(lb_ref[0, rb, :], reverse=True)

        for flag, pred in ((True, safe), (False, jnp.logical_not(safe))):
            @pl.when(pred)
            def _(flag=flag, rf=rf, rb=rb, b_fw=b_fw, b_bw=b_bw):
                _gla_step(qf_ref, kf_ref, vf_ref, of_ref, sf_ref, rows=rf, reverse=False, safe=flag, b_all=b_fw)
                _gla_step(qb_ref, kb_ref, vb_ref, ob_ref, sb_ref, rows=rb, reverse=True, safe=flag, b_all=b_bw)


def _gla(q, l_fw, k_fw, l_bw, k_bw, v, chunk):
    bsz, seq, d = q.shape
    nc = seq // chunk
    n_heads = d // HG_HEAD_DIM
    fw = pl.BlockSpec((1, chunk, d), lambda b, c: (b, c, 0))
    bw = pl.BlockSpec((1, chunk, d), lambda b, c: (b, nc - 1 - c, 0))
    state = pltpu.VMEM((n_heads, HG_HEAD_DIM, HG_HEAD_DIM), F32)
    return pl.pallas_call(
        _gla_kernel,
        grid=(bsz, nc),
        in_specs=[fw] * 4 + [bw] * 4,
        out_specs=[fw, bw],
        out_shape=[jax.ShapeDtypeStruct((bsz, seq, d), BF16)] * 2,
        scratch_shapes=[state, state],
        compiler_params=_cparams("parallel", "arbitrary"),
        name="hgrn_gla",
    )(q, l_fw, k_fw, v, q, l_bw, k_bw, v)


def _hgrn_out_kernel(of_ref, ob_ref, g_ref, h_ref, nw_ref, w_ref, out_ref, slab_ref):
    d = h_ref.shape[1]
    nw = nw_ref[...]
    parts = []
    for c0 in range(0, d, HG_HEAD_DIM):
        hs = slice(c0, c0 + HG_HEAD_DIM)
        o = of_ref[:, hs].astype(F32) + ob_ref[:, hs].astype(F32)
        y = _rmsnorm(o, nw) * g_ref[:, hs].astype(F32)
        parts.append(y.astype(BF16))
    y = jnp.concatenate(parts, axis=1)
    _deinterleave_add(h_ref, slab_ref, out_ref, _dot(y, w_ref[...]))


def _hgrn_out(o_fw, o_bw, g, h2, norm_w, w_out_bf, j, tm):
    t, d = h2.shape
    tile = pl.BlockSpec((tm, d), lambda i: (i, 0))
    return pl.pallas_call(
        _hgrn_out_kernel,
        grid=(t // tm,),
        in_specs=[tile, tile, tile, tile, _const_spec((1, HG_HEAD_DIM)), _layer_spec(w_out_bf, j)],
        out_specs=tile,
        out_shape=jax.ShapeDtypeStruct((t, d), F32),
        scratch_shapes=[pltpu.VMEM((d // LANES, tm, LANES), F32)],
        compiler_params=_cparams("parallel"),
        name="hgrn_out",
    )(o_fw, o_bw, g, h2, norm_w.reshape(1, HG_HEAD_DIM), w_out_bf)


def _halo_rows(hp_ref, hn_ref, nw):
    up = _rmsnorm(hp_ref[0], nw)
    un = _rmsnorm(hn_ref[0], nw)
    return jnp.concatenate([up, un], axis=0).astype(BF16)


def _halo_inside():
    i = pl.program_id(1)
    r = _iota2((2 * SUBLANES, 1), 0)
    has_prev = (i > 0).astype(F32)
    has_next = (i < pl.num_programs(1) - 1).astype(F32)
    return jnp.where(r < SUBLANES, has_prev, has_next) > 0.5


def _shifted(e_ref, x, xh, halo, n_rows):
    e_ref[0:SUBLANES, :] = xh[0:SUBLANES]
    e_ref[SUBLANES:SUBLANES + n_rows, :] = x
    e_ref[SUBLANES + n_rows:, :] = xh[SUBLANES:]
    return [x if j == halo else e_ref[pl.ds(SUBLANES - halo + j, n_rows), :] for j in range(2 * halo + 1)]


def _ffn_kernel(h_ref, hp_ref, hn_ref, nw_ref, wg_ref, wv_ref, cw_ref, cb_ref, wo_ref, fw_ref,
                out_ref, a_ref, e_ref, *, col_chunk, final_norm):
    tm = h_ref.shape[1]
    d_ff = wg_ref.shape[1]
    nw = nw_ref[...]
    hx = h_ref[0]
    u = _rmsnorm(hx, nw).astype(BF16)
    uh = _halo_rows(hp_ref, hn_ref, nw)
    inside = _halo_inside()
    for c0 in range(0, d_ff, col_chunk):
        cols = slice(c0, c0 + col_chunk)
        gate = _dot(u, wg_ref[:, cols])
        gh = jnp.where(inside, _dot(uh, wg_ref[:, cols]), 0.0)
        taps = _shifted(e_ref, gate, gh, FFN_CONV_HALO, tm)
        half_conv = 0.5 * cb_ref[:, cols]
        for j, tap in enumerate(taps):
            half_conv = half_conv + (0.5 * cw_ref[j:j + 1, cols]) * tap
        val = _dot(u, wv_ref[:, cols])
        a_ref[:, cols] = (_silu_of_twice(half_conv) * val).astype(BF16)
    y = hx + _dot(a_ref[...], wo_ref[...])
    if final_norm:
        y = _rmsnorm(y, fw_ref[...])
    out_ref[0] = y


def _halo_specs(tm, d, seq):
    nb = tm // SUBLANES
    n_blocks = seq // SUBLANES
    prev = pl.BlockSpec((1, SUBLANES, d), lambda b, i: (b, jnp.maximum(i * nb - 1, 0), 0))
    nxt = pl.BlockSpec((1, SUBLANES, d), lambda b, i: (b, jnp.minimum((i + 1) * nb, n_blocks - 1), 0))
    return prev, nxt


def _ffn(h3, norm_w, w_in_bf, conv_w, conv_b, wo_bf, layer, final_w, tm, final_norm):
    bsz, seq, d = h3.shape
    d_ff = wo_bf.shape[1]
    col_chunk = 256
    tile = pl.BlockSpec((1, tm, d), lambda b, i: (b, i, 0))
    prev, nxt = _halo_specs(tm, d, seq)
    return pl.pallas_call(
        functools.partial(_ffn_kernel, col_chunk=col_chunk, final_norm=final_norm),
        grid=(bsz, seq // tm),
        in_specs=[tile, prev, nxt, _const_spec((1, d)), _layer_spec(w_in_bf, layer, 2, 0),
                  _layer_spec(w_in_bf, layer, 2, 1), _const_spec(conv_w.shape), _const_spec((1, d_ff)),
                  _layer_spec(wo_bf, layer), _const_spec((1, d))],
        out_specs=tile,
        out_shape=jax.ShapeDtypeStruct(h3.shape, F32),
        scratch_shapes=[pltpu.VMEM((tm, d_ff), BF16), pltpu.VMEM((tm + 2 * SUBLANES, col_chunk), F32)],
        compiler_params=_cparams("parallel", "parallel"),
        name="convglu_final" if final_norm else "convglu",
    )(h3, h3, h3, norm_w.reshape(1, d), w_in_bf, w_in_bf, conv_w, conv_b.reshape(1, d_ff), wo_bf,
      final_w.reshape(1, d))


def _interleaved_taps(x, xh, halo):
    tm, w = x.shape
    nv = CHUNK // SUBLANES
    n_chunks = tm // CHUNK
    sub = _iota2((SUBLANES, w), 0)
    tile = lambda g, j: x[g * CHUNK + j * SUBLANES: g * CHUNK + (j + 1) * SUBLANES]

    def after(g, j):
        if g + 1 < n_chunks:
            return tile(g + 1, j)
        nxt = xh[SUBLANES:]
        return nxt if j == 0 else pltpu.roll(nxt, SUBLANES - j, 0)

    def before(g, j):
        if g > 0:
            return tile(g - 1, j)
        prv = xh[:SUBLANES]
        return prv if j == nv - 1 else pltpu.roll(prv, nv - 1 - j, 0)

    taps = [[] for _ in range(2 * halo + 1)]
    for g in range(n_chunks):
        lead = [pltpu.roll(jnp.where(sub == SUBLANES - 1, before(g, nv - halo + m), tile(g, nv - halo + m)), 1, 0)
                for m in range(halo)]
        tail = [pltpu.roll(jnp.where(sub == 0, after(g, m), tile(g, m)), SUBLANES - 1, 0) for m in range(halo)]
        ext = jnp.concatenate(lead + [x[g * CHUNK:(g + 1) * CHUNK]] + tail, axis=0)
        for j in range(2 * halo + 1):
            taps[j].append(ext[j * SUBLANES: j * SUBLANES + CHUNK])
    return [jnp.concatenate(t, axis=0) for t in taps]


def _ssm_proj_kernel(h_ref, hp_ref, hn_ref, nw_ref, wz_ref, wx_ref, wbc_ref, wdt_ref, cw_ref, cb_ref, dtb_ref,
                     z_ref, xs_ref, bm_ref, cm_ref, dt_ref, slab_ref, *, col_chunk):
    d_inner = z_ref.shape[2]
    gn = bm_ref.shape[2]
    nw = nw_ref[...]
    u = _rmsnorm(_interleave_rows(h_ref[0], slab_ref), nw).astype(BF16)
    uh = _halo_rows(hp_ref, hn_ref, nw)
    inside = _halo_inside()
    for c0 in range(0, d_inner, col_chunk):
        cols = slice(c0, c0 + col_chunk)
        z_ref[0, :, cols] = _silu(_dot(u, wz_ref[:, cols])).astype(BF16)
    for c0 in range(0, d_inner + 2 * gn, col_chunk):
        cols = slice(c0, c0 + col_chunk)
        w = wx_ref[:, cols] if c0 < d_inner else wbc_ref[:, c0 - d_inner: c0 - d_inner + col_chunk]
        x = _dot(u, w)
        xh = jnp.where(inside, _dot(uh, w), 0.0)
        taps = _interleaved_taps(x, xh, SSM_CONV_HALO)
        half_conv = 0.5 * cb_ref[:, cols]
        for j, tap in enumerate(taps):
            half_conv = half_conv + (0.5 * cw_ref[j:j + 1, cols]) * tap
        act = _silu_of_twice(half_conv).astype(BF16)
        if c0 < d_inner:
            xs_ref[0, :, cols] = act
        elif c0 < d_inner + gn:
            bm_ref[0, :, c0 - d_inner: c0 - d_inner + col_chunk] = act
        else:
            cm_ref[0, :, c0 - d_inner - gn: c0 - d_inner - gn + col_chunk] = act
    dt_ref[0] = _softplus(_dot(u, wdt_ref[...]) + dtb_ref[...])


def _ssm_proj(h3, norm_w, w_in_bf, j, wdt_bf, conv_w, conv_b, dt_bias_row, d_inner, tm):
    bsz, seq, d = h3.shape
    conv_dim = conv_w.shape[1]
    gn = (conv_dim - d_inner) // 2
    assert 2 * gn == d_inner
    col_chunk = 512
    tile = pl.BlockSpec((1, tm, d), lambda b, i: (b, i, 0))
    prev, nxt = _halo_specs(tm, d, seq)
    out_tile = lambda w: pl.BlockSpec((1, tm, w), lambda b, i: (b, i, 0))
    tok = lambda w, dt: jax.ShapeDtypeStruct((bsz, seq, w), dt)
    return pl.pallas_call(
        functools.partial(_ssm_proj_kernel, col_chunk=col_chunk),
        grid=(bsz, seq // tm),
        in_specs=[tile, prev, nxt, _const_spec((1, d)), _wide_layer_spec(w_in_bf, j, d_inner, 0),
                  _wide_layer_spec(w_in_bf, j, d_inner, 1), _wide_layer_spec(w_in_bf, j, d_inner, 2),
                  _const_spec(wdt_bf.shape), _const_spec(conv_w.shape), _const_spec((1, conv_dim)),
                  _const_spec((1, LANES))],
        out_specs=[out_tile(d_inner), out_tile(d_inner), out_tile(gn), out_tile(gn), out_tile(LANES)],
        out_shape=[tok(d_inner, BF16), tok(d_inner, BF16), tok(gn, BF16), tok(gn, BF16), tok(LANES, F32)],
        scratch_shapes=[pltpu.VMEM((d // LANES, tm, LANES), F32)],
        compiler_params=_cparams("parallel", "parallel"),
        name="ssm_proj",
    )(h3, h3, h3, norm_w.reshape(1, d), w_in_bf, w_in_bf, w_in_bf, wdt_bf, conv_w, conv_b.reshape(1, conv_dim),
      dt_bias_row)


def _ssd_step(xs_ref, bm_ref, cm_ref, dt_ref, z_ref, alog_ref, y_ref, st_ref, *, rows, reverse, skip_ref=None):
    c = CHUNK
    n_groups = st_ref.shape[0]
    n_state = st_ref.shape[1]
    gw = st_ref.shape[2]
    heads = n_groups * (gw // SSM_HEAD_DIM)
    h0 = heads if reverse else 0
    row = _chunk_token(_iota2((c, c), 0))
    col = _chunk_token(_iota2((c, c), 1))
    lane = _iota2((c, LANES), 1)
    keep = (row <= col) if reverse else (row >= col)
    last = 0 if reverse else c - 1

    dt = dt_ref[0, rows, :]
    la = dt * (-LOG2E * jnp.exp(alog_ref[...]))
    a_col = _chunk_running_sum(la, reverse)
    a_row = a_col.T
    a_src = a_row - jnp.log(dt.T) * LOG2E
    w_row_bf = jnp.exp2(a_row[:, last:last + 1] - a_src).astype(BF16)

    hpg = gw // SSM_HEAD_DIM
    lane_g = _iota2((c, gw), 1)
    for g in range(n_groups):
        bg = bm_ref[0, rows, g * n_state:(g + 1) * n_state]
        cg = cm_ref[0, rows, g * n_state:(g + 1) * n_state]
        bt = bg.astype(F32).T.astype(BF16)
        st = st_ref[g]
        cg_out = _dot(cg, jnp.concatenate([bt, st.astype(BF16)], axis=1))
        cb = cg_out[:, :c].astype(BF16)
        y_in = cg_out[:, c:]
        xg = xs_ref[0, rows, g * gw:(g + 1) * gw]
        zero = jnp.zeros_like(xg)
        ws, bws, bcs, x_blocks = [], [], [], []
        for j in range(hpg):
            hh = h0 + g * hpg + j
            bc = jnp.broadcast_to(a_col[:, hh:hh + 1], (c, c))
            decay = jnp.exp2(jnp.where(keep, bc - a_src[hh:hh + 1, :], -jnp.inf))
            ws.append(cb * decay.astype(BF16))
            bws.append(bt * w_row_bf[hh:hh + 1, :])
            bcs.append(bc)
            x_blocks.append(jnp.where((lane_g >= j * SSM_HEAD_DIM) & (lane_g < (j + 1) * SSM_HEAD_DIM), xg, zero))
        lhs = jnp.concatenate([jnp.concatenate(ws, axis=1), jnp.concatenate(bws, axis=1)], axis=0)
        res = _dot(lhs, jnp.concatenate(x_blocks, axis=0))
        e_parts = []
        for p in range(gw // LANES):
            e_parts.append(jnp.exp2(jnp.where(lane < SSM_HEAD_DIM, bcs[2 * p], bcs[2 * p + 1])))
        e_t = jnp.concatenate(e_parts, axis=1)
        y = res[:c] + y_in * e_t
        if skip_ref is not None:
            y = y + xg.astype(F32) * skip_ref[:, g * gw:(g + 1) * gw]
        y_ref[0, rows, g * gw:(g + 1) * gw] = (y * z_ref[0, rows, g * gw:(g + 1) * gw].astype(F32)).astype(y_ref.dtype)
        st_ref[g] = st * e_t[last:last + 1, :] + res[c:]


def _ssd_kernel(xf_ref, bf_ref, cf_ref, dtf_ref, zf_ref, xb_ref, bb_ref, cb_ref, dtb_ref, zb_ref, alog_ref, dsk_ref,
                yf_ref, yb_ref, sf_ref, sb_ref):
    @pl.when(pl.program_id(1) == 0)
    def _():
        sf_ref[...] = jnp.zeros_like(sf_ref)
        sb_ref[...] = jnp.zeros_like(sb_ref)

    n_sub = xf_ref.shape[1] // CHUNK
    for i in range(n_sub):
        rf = slice(i * CHUNK, (i + 1) * CHUNK)
        rb = slice((n_sub - 1 - i) * CHUNK, (n_sub - i) * CHUNK)
        _ssd_step(xf_ref, bf_ref, cf_ref, dtf_ref, zf_ref, alog_ref, yf_ref, sf_ref, rows=rf, reverse=False,
                  skip_ref=dsk_ref)
        _ssd_step(xb_ref, bb_ref, cb_ref, dtb_ref, zb_ref, alog_ref, yb_ref, sb_ref, rows=rb, reverse=True)


def _ssd(xs, bm, cm, dt, z, alog_row, d_skip_row, chunk):
    bsz, seq, d_inner = xs.shape
    gn = bm.shape[2]
    nc = seq // chunk
    n_state = gn // SSM_GROUPS
    fw = lambda w: pl.BlockSpec((1, chunk, w), lambda b, c: (b, c, 0))
    bw = lambda w: pl.BlockSpec((1, chunk, w), lambda b, c: (b, nc - 1 - c, 0))
    state = pltpu.VMEM((SSM_GROUPS, n_state, d_inner // SSM_GROUPS), F32)
    return pl.pallas_call(
        _ssd_kernel,
        grid=(bsz, nc),
        in_specs=[fw(d_inner), fw(gn), fw(gn), fw(LANES), fw(d_inner), bw(d_inner), bw(gn), bw(gn), bw(LANES),
                  bw(d_inner), _const_spec((1, LANES)), _const_spec((1, d_inner))],
        out_specs=[fw(d_inner), bw(d_inner)],
        out_shape=[jax.ShapeDtypeStruct(xs.shape, BF16)] * 2,
        scratch_shapes=[state, state],
        compiler_params=_cparams("parallel", "arbitrary"),
        name="ssm_ssd",
    )(xs, bm, cm, dt, z, xs, bm, cm, dt, z, alog_row, d_skip_row)


def _ssm_out_kernel(yf_ref, yb_ref, h_ref, nw_ref, w_ref, out_ref, slab_ref, *, group_w):
    d_inner = yf_ref.shape[1]
    parts = []
    for c0 in range(0, d_inner, group_w):
        cs = slice(c0, c0 + group_w)
        y = yf_ref[:, cs].astype(F32) + yb_ref[:, cs].astype(F32)
        parts.append(_rmsnorm(y, nw_ref[:, cs]).astype(BF16))
    y = jnp.concatenate(parts, axis=1)
    _deinterleave_add(h_ref, slab_ref, out_ref, _dot(y, w_ref[...]))


def _ssm_out(y_fw, y_bw, h2, norm_w, w_out_bf, j, tm):
    t, d = h2.shape
    d_inner = y_fw.shape[1]
    wide = pl.BlockSpec((tm, d_inner), lambda i: (i, 0))
    tile = pl.BlockSpec((tm, d), lambda i: (i, 0))
    return pl.pallas_call(
        functools.partial(_ssm_out_kernel, group_w=d_inner // SSM_GROUPS),
        grid=(t // tm,),
        in_specs=[wide, wide, tile, _const_spec((1, d_inner)), _layer_spec(w_out_bf, j)],
        out_specs=tile,
        out_shape=jax.ShapeDtypeStruct((t, d), F32),
        scratch_shapes=[pltpu.VMEM((d // LANES, tm, LANES), F32)],
        compiler_params=_cparams("parallel"),
        name="ssm_out",
    )(y_fw, y_bw, h2, norm_w.reshape(1, d_inner), w_out_bf)


def kernel(x, norm1_w, norm2_w, a_w_in, a_lb_logits, a_norm_w, a_w_out, b_w_in, b_conv_w, b_conv_b,
           b_dt_bias, b_a_log, b_d_skip, b_norm_w, b_w_out, ffn_w_in, ffn_conv_w, ffn_conv_b, ffn_w_out,
           final_norm_w):
    bsz, seq, d = x.shape
    depth = norm1_w.shape[0]
    t = bsz * seq
    tm = min(512, seq)
    d_inner = b_norm_w.shape[1]
    n_ssm_heads = b_dt_bias.shape[2]
    assert seq % tm == 0 and tm % CHUNK == 0 and seq % (REC_STEP_CHUNKS * CHUNK) == 0
    assert 2 * n_ssm_heads <= LANES and d_inner == n_ssm_heads * SSM_HEAD_DIM

    bf = lambda w: w.astype(BF16)
    a_w_in, a_w_out, b_w_out, ffn_w_in, ffn_w_out = map(bf, (a_w_in, a_w_out, b_w_out, ffn_w_in, ffn_w_out))
    n_zxbc = d_inner + b_conv_w.shape[2]
    b_w_dt = bf(b_w_in[:, :, n_zxbc:])
    b_w_in = bf(lax.optimization_barrier(b_w_in[:, :, :n_zxbc]))
    h = x
    for i in range(depth):
        j = i // 2
        if i % 2 == 0:
            q, l_fw, k_fw, l_bw, k_bw, v, g = _hgrn_proj(h.reshape(t, d), norm1_w[i], a_w_in, j, a_lb_logits, i, tm)
            r3 = lambda a: a.reshape(bsz, seq, d)
            o_fw, o_bw = _gla(r3(q), r3(l_fw), r3(k_fw), r3(l_bw), r3(k_bw), r3(v), REC_STEP_CHUNKS * CHUNK)
            h = _hgrn_out(o_fw.reshape(t, d), o_bw.reshape(t, d), g, h.reshape(t, d), a_norm_w[j], a_w_out, j,
                          min(2 * tm, seq)).reshape(bsz, seq, d)
        else:
            pad = LANES - 2 * n_ssm_heads
            wdt = jnp.pad(b_w_dt[j], ((0, 0), (0, pad)))
            dt_bias_row = jnp.pad(b_dt_bias[j].reshape(1, -1), ((0, 0), (0, pad)))
            alog_row = jnp.pad(b_a_log[j].reshape(1, -1), ((0, 0), (0, pad)))
            z, xs, bm, cm, dt = _ssm_proj(h, norm1_w[i], b_w_in, j, wdt, b_conv_w[j], b_conv_b[j], dt_bias_row,
                                          d_inner, tm)
            d_skip_row = jnp.repeat(b_d_skip[j], SSM_HEAD_DIM).reshape(1, d_inner)
            y_fw, y_bw = _ssd(xs, bm, cm, dt, z, alog_row, d_skip_row, REC_STEP_CHUNKS * CHUNK)
            f2 = lambda a: a.reshape(t, d_inner)
            h = _ssm_out(f2(y_fw), f2(y_bw), h.reshape(t, d), b_norm_w[j], b_w_out, j,
                         min(2 * tm, seq)).reshape(bsz, seq, d)
        h = _ffn(h, norm2_w[i], ffn_w_in, ffn_conv_w[i], ffn_conv_b[i], ffn_w_out, i, final_norm_w,
                 min(2 * tm, seq), final_norm=(i == depth - 1))
    return h
```
